```python
import jax, jax.numpy as jnp
from jax import lax
import numpy as np

D_MODEL = 4096
BATCH = 4
SEQ = 2048
DEPTH = 4
DEC_BATCH = 128
DEC_SEQ = 8
PAST_LEN = 8192
PAGE_SIZE = 128

MLA_HEADS = 12
Q_LORA = 768
KV_LORA = 256
QK_NOPE = 128
QK_ROPE = 64
MLA_V = 128
ROPE_THETA = 10000.0
MLA_SCALE = (QK_NOPE + QK_ROPE) ** -0.5
FOX_HEADS = 12
FOX_HD = 128
FOX_SCALE = FOX_HD ** -0.5
FORGET_BIAS = 4.0
MEM_HEADS = 4
MEM_HD = 256
MEM_SLOTS = 256
MEM_SCALE = MEM_HD ** -0.5
MLA_W = MLA_HEADS * MLA_V
FOX_W = FOX_HEADS * FOX_HD
MEM_W = MEM_HEADS * MEM_HD
MIX_W = MLA_W + FOX_W + MEM_W
IN_SIZES = (Q_LORA, KV_LORA, QK_ROPE, FOX_W, FOX_HD, FOX_HD, FOX_HEADS, MEM_W, MIX_W)
N_IN = sum(IN_SIZES)
Q_BLOCK = 128
ALPHA = (2 * DEPTH) ** 0.25
BETA = (8 * DEPTH) ** -0.25
RMS_EPS = 1e-6
LN_EPS = 1e-5

kernel_name = 'hybrid_mla_fox_mem_decoder_step'


def rmsnorm(x, g):
    xf = x.astype(jnp.float32)
    y = xf * lax.rsqrt(jnp.mean(xf * xf, axis=-1, keepdims=True) + RMS_EPS)
    return (y * g.astype(jnp.float32)).astype(x.dtype)


def layernorm(x, g, b):
    xf = x.astype(jnp.float32)
    mu = jnp.mean(xf, axis=-1, keepdims=True)
    xc = xf - mu
    var = jnp.mean(xc * xc, axis=-1, keepdims=True)
    y = xc * lax.rsqrt(var + LN_EPS) * g.astype(jnp.float32) + b.astype(jnp.float32)
    return y.astype(x.dtype)


def rope(x, pos):
    half = QK_ROPE // 2
    inv = ROPE_THETA ** (-jnp.arange(half, dtype=jnp.float32) / half)
    ang = pos.astype(jnp.float32)[:, None] * inv[None, :]
    cos = jnp.cos(ang)[None, :, None, :]
    sin = jnp.sin(ang)[None, :, None, :]
    x1 = x[..., :half].astype(jnp.float32)
    x2 = x[..., half:].astype(jnp.float32)
    return jnp.concatenate([x1 * cos - x2 * sin, x2 * cos + x1 * sin], axis=-1).astype(x.dtype)


def project(x, pos, w_in, b_f, q_a_norm, w_q_up, kv_a_norm):
    B, T, _ = x.shape
    h = jnp.einsum('btd,dn->btn', x, w_in)
    splits = np.cumsum(IN_SIZES)[:-1].tolist()
    q_a, kv_a, k_pe, fq, fk, fv, f_logit, mq, gate = jnp.split(h, splits, axis=-1)
    q = jnp.einsum('btl,ln->btn', rmsnorm(q_a, q_a_norm), w_q_up).reshape(B, T, MLA_HEADS, QK_NOPE + QK_ROPE)
    q_nope = q[..., :QK_NOPE]
    q_rope = rope(q[..., QK_NOPE:], pos)
    ckv = rmsnorm(kv_a, kv_a_norm)
    krope = rope(k_pe[:, :, None, :], pos)[:, :, 0, :]
    fq = fq.reshape(B, T, FOX_HEADS, FOX_HD)
    logf = jax.nn.log_sigmoid((f_logit + b_f).astype(jnp.float32)).astype(x.dtype)
    mq = mq.reshape(B, T, MEM_HEADS, MEM_HD)
    return q_nope, q_rope, ckv, krope, fq, fk, fv, logf, mq, gate


def mla_attend(q_nope, q_rope, ckv, krope, mask, w_kv_up):
    B, Tq = q_nope.shape[:2]
    w_uk = w_kv_up[..., :QK_NOPE]
    w_uv = w_kv_up[..., QK_NOPE:]
    q_lat = jnp.einsum('bthd,chd->bthc', q_nope, w_uk)
    s = jnp.einsum('bthc,bsc->bhts', q_lat, ckv) + jnp.einsum('bthr,bsr->bhts', q_rope, krope)
    s = jnp.where(mask, s.astype(jnp.float32) * MLA_SCALE, -jnp.inf)
    p = jax.nn.softmax(s, axis=-1).astype(ckv.dtype)
    o_lat = jnp.einsum('bhts,bsc->bthc', p, ckv)
    o = jnp.einsum('bthc,chd->bthd', o_lat, w_uv)
    return o.reshape(B, Tq, MLA_W)


def fox_attend(fq, fk, fv, cq, ck, mask):
    B, Tq = fq.shape[:2]
    s = jnp.einsum('bthd,bsd->bhts', fq, fk).astype(jnp.float32) * FOX_SCALE
    s = s + jnp.swapaxes(cq, 1, 2)[:, :, :, None] - jnp.swapaxes(ck, 1, 2)[:, :, None, :]
    s = jnp.where(mask, s, -jnp.inf)
    p = jax.nn.softmax(s, axis=-1).astype(fv.dtype)
    return jnp.einsum('bhts,bsd->bthd', p, fv).reshape(B, Tq, FOX_W)


def mem_attend(mq, mk, mv):
    B, T = mq.shape[:2]
    s = jnp.einsum('bthd,bmhd->bhtm', mq, mk).astype(jnp.float32) * MEM_SCALE
    p = jax.nn.softmax(s, axis=-1).astype(mv.dtype)
    return jnp.einsum('bhtm,bmhd->bthd', p, mv).reshape(B, T, MEM_W)


def mix_out(x, o, gate, w_out, ln_g, ln_b):
    y = jnp.einsum('btm,md->btd', o * jax.nn.silu(gate), w_out)
    return layernorm(ALPHA * x + y, ln_g, ln_b)


def to_blocks(a):
    B, T = a.shape[:2]
    return jnp.moveaxis(a.reshape(B, T // Q_BLOCK, Q_BLOCK, *a.shape[2:]), 1, 0)


def from_blocks(a):
    NB, B, QB = a.shape[:3]
    return jnp.moveaxis(a, 0, 1).reshape(B, NB * QB, *a.shape[3:])


def prompt_self_attention(q_nope, q_rope, ckv, krope, fq, fk, fv, c, w_kv_up):
    T = ckv.shape[1]
    k_pos = jnp.arange(T)

    def block(args):
        i, qn_b, qr_b, fq_b, cq_b = args
        q_pos = i * Q_BLOCK + jnp.arange(Q_BLOCK)
        mask = q_pos[:, None] >= k_pos[None, :]
        return jnp.concatenate([mla_attend(qn_b, qr_b, ckv, krope, mask, w_kv_up),
                                fox_attend(fq_b, fk, fv, cq_b, c, mask)], axis=-1)

    outs = lax.map(block, (jnp.arange(T // Q_BLOCK), to_blocks(q_nope), to_blocks(q_rope),
                           to_blocks(fq), to_blocks(c)))
    return from_blocks(outs)


def gather_pages(pool, page_table):
    g = pool[page_table]
    return g.reshape(page_table.shape[0], page_table.shape[1] * pool.shape[1], *pool.shape[2:])


def setup_inputs(seed: int = 0) -> dict:
    key = jax.random.key(seed)
    ks = jax.random.split(key, 24)
    n_pages = PAST_LEN // PAGE_SIZE
    n_pool = (DEC_BATCH * n_pages * 5) // 4

    def nrm(k, shape, scale):
        return jax.random.normal(k, shape, jnp.float32) * scale

    pool_shape = (DEPTH, n_pool, PAGE_SIZE)
    x_prompt = nrm(ks[0], (BATCH, SEQ, D_MODEL), 1.0)
    x_sample = nrm(ks[1], (DEC_BATCH, DEC_SEQ, D_MODEL), 1.0)
    cache_mla_ckv = nrm(ks[2], pool_shape + (KV_LORA,), 1.0)
    cache_mla_krope = nrm(ks[3], pool_shape + (QK_ROPE,), 1.0)
    cache_fox_k = nrm(ks[4], pool_shape + (FOX_HD,), 1.0)
    cache_fox_v = nrm(ks[5], pool_shape + (FOX_HD,), 1.0)
    cache_fox_logf = jax.nn.log_sigmoid(FORGET_BIAS + nrm(ks[6], pool_shape + (FOX_HEADS,), 1.0))
    cache_mem_k = nrm(ks[7], (DEPTH, DEC_BATCH, MEM_SLOTS, MEM_HEADS, MEM_HD), 1.0)
    cache_mem_v = nrm(ks[8], (DEPTH, DEC_BATCH, MEM_SLOTS, MEM_HEADS, MEM_HD), 1.0)
    page_table = jax.random.permutation(ks[9], n_pool)[:DEC_BATCH * n_pages].reshape(DEC_BATCH, n_pages).astype(jnp.int32)
    mem_prompt = nrm(ks[10], (BATCH, MEM_SLOTS, D_MODEL), 1.0)
    w_in = nrm(ks[11], (DEPTH, D_MODEL, N_IN), D_MODEL ** -0.5)
    b_f = FORGET_BIAS + nrm(ks[12], (DEPTH, FOX_HEADS), 0.1)
    q_a_norm = 1.0 + nrm(ks[13], (DEPTH, Q_LORA), 0.02)
    w_q_up = nrm(ks[14], (DEPTH, Q_LORA, MLA_HEADS * (QK_NOPE + QK_ROPE)), Q_LORA ** -0.5)
    kv_a_norm = 1.0 + nrm(ks[15], (DEPTH, KV_LORA), 0.02)
    w_kv_up = nrm(ks[16], (DEPTH, KV_LORA, MLA_HEADS, QK_NOPE + MLA_V), KV_LORA ** -0.5)
    w_mem_k = nrm(ks[17], (DEPTH, D_MODEL, MEM_W), D_MODEL ** -0.5)
    w_mem_v = nrm(ks[18], (DEPTH, D_MODEL, MEM_W), D_MODEL ** -0.5)
    w_out = nrm(ks[19], (DEPTH, MIX_W, D_MODEL), (MIX_W ** -0.5) * BETA)
    ln_g = 1.0 + nrm(ks[20], (DEPTH, D_MODEL), 0.02)
    ln_b = nrm(ks[21], (DEPTH, D_MODEL), 0.02)
    return {'x_prompt': x_prompt, 'x_sample': x_sample,
            'cache_mla_ckv': cache_mla_ckv, 'cache_mla_krope': cache_mla_krope,
            'cache_fox_k': cache_fox_k, 'cache_fox_v': cache_fox_v, 'cache_fox_logf': cache_fox_logf,
            'cache_mem_k': cache_mem_k, 'cache_mem_v': cache_mem_v,
            'page_table': page_table, 'mem_prompt': mem_prompt,
            'w_in': w_in, 'b_f': b_f, 'q_a_norm': q_a_norm, 'w_q_up': w_q_up,
            'kv_a_norm': kv_a_norm, 'w_kv_up': w_kv_up, 'w_mem_k': w_mem_k, 'w_mem_v': w_mem_v,
            'w_out': w_out, 'ln_g': ln_g, 'ln_b': ln_b}


def reference(x_prompt, x_sample, cache_mla_ckv, cache_mla_krope, cache_fox_k, cache_fox_v,
              cache_fox_logf, cache_mem_k, cache_mem_v, page_table, mem_prompt,
              w_in, b_f, q_a_norm, w_q_up, kv_a_norm, w_kv_up, w_mem_k, w_mem_v,
              w_out, ln_g, ln_b):
    bp, tp = x_prompt.shape[:2]
    pos_p = jnp.arange(tp)
    h = x_prompt
    p_ckv, p_kr, p_fk, p_fv, p_lf, p_mk, p_mv = [], [], [], [], [], [], []
    for l in range(DEPTH):
        qn, qr, ckv, kr, fq, fk, fv, logf, mq, gate = project(h, pos_p, w_in[l], b_f[l], q_a_norm[l], w_q_up[l], kv_a_norm[l])
        c = jnp.cumsum(logf.astype(jnp.float32), axis=1)
        o_self = prompt_self_attention(qn, qr, ckv, kr, fq, fk, fv, c, w_kv_up[l])
        mk = jnp.einsum('bmd,dn->bmn', mem_prompt, w_mem_k[l]).reshape(bp, MEM_SLOTS, MEM_HEADS, MEM_HD)
        mv = jnp.einsum('bmd,dn->bmn', mem_prompt, w_mem_v[l]).reshape(bp, MEM_SLOTS, MEM_HEADS, MEM_HD)
        o = jnp.concatenate([o_self, mem_attend(mq, mk, mv)], axis=-1)
        h = mix_out(h, o, gate, w_out[l], ln_g[l], ln_b[l])
        p_ckv.append(ckv); p_kr.append(kr); p_fk.append(fk); p_fv.append(fv)
        p_lf.append(logf); p_mk.append(mk); p_mv.append(mv)
    y_prompt = h

    ts = x_sample.shape[1]
    past = page_table.shape[1] * PAGE_SIZE
    pos_s = past + jnp.arange(ts)
    mask_s = pos_s[:, None] >= jnp.arange(past + ts)[None, :]
    h = x_sample
    s_ckv, s_kr, s_fk, s_fv, s_lf = [], [], [], [], []
    for l in range(DEPTH):
        qn, qr, ckv, kr, fq, fk, fv, logf, mq, gate = project(h, pos_s, w_in[l], b_f[l], q_a_norm[l], w_q_up[l], kv_a_norm[l])
        ckv_all = jnp.concatenate([gather_pages(cache_mla_ckv[l], page_table), ckv], axis=1)
        kr_all = jnp.concatenate([gather_pages(cache_mla_krope[l], page_table), kr], axis=1)
        fk_all = jnp.concatenate([gather_pages(cache_fox_k[l], page_table), fk], axis=1)
        fv_all = jnp.concatenate([gather_pages(cache_fox_v[l], page_table), fv], axis=1)
        lf_all = jnp.concatenate([gather_pages(cache_fox_logf[l], page_table), logf], axis=1)
        c = jnp.cumsum(lf_all.astype(jnp.float32), axis=1)
        o = jnp.concatenate([mla_attend(qn, qr, ckv_all, kr_all, mask_s, w_kv_up[l]),
                             fox_attend(fq, fk_all, fv_all, c[:, past:], c, mask_s),
                             mem_attend(mq, cache_mem_k[l], cache_mem_v[l])], axis=-1)
        h = mix_out(h, o, gate, w_out[l], ln_g[l], ln_b[l])
        s_ckv.append(ckv); s_kr.append(kr); s_fk.append(fk); s_fv.append(fv); s_lf.append(logf)
    y_sample = h

    p_ckv = jnp.stack(p_ckv); p_krope = jnp.stack(p_kr)
    p_fox_k = jnp.stack(p_fk); p_fox_v = jnp.stack(p_fv); p_fox_logf = jnp.stack(p_lf)
    p_mem_k = jnp.stack(p_mk); p_mem_v = jnp.stack(p_mv)
    s_ckv = jnp.stack(s_ckv); s_krope = jnp.stack(s_kr)
    s_fox_k = jnp.stack(s_fk); s_fox_v = jnp.stack(s_fv); s_fox_logf = jnp.stack(s_lf)
    return (y_prompt, y_sample, p_ckv, p_krope, p_fox_k, p_fox_v, p_fox_logf, p_mem_k, p_mem_v,
            s_ckv, s_krope, s_fox_k, s_fox_v, s_fox_logf)
```

```python
import functools

import jax
import jax.numpy as jnp
from jax import lax
from jax.experimental import pallas as pl
from jax.experimental.pallas import tpu as pltpu

F32 = jnp.float32
BF16 = jnp.bfloat16

MLA_HEADS = 12
Q_LORA = 768
KV_LORA = 256
QK_NOPE = 128
QK_ROPE = 64
MLA_V = 128
ROPE_THETA = 10000.0
MLA_SCALE = (QK_NOPE + QK_ROPE) ** -0.5
FOX_HEADS = 12
FOX_HD = 128
FOX_SCALE = FOX_HD ** -0.5
MEM_HEADS = 4
MEM_HD = 256
MEM_SCALE = MEM_HD ** -0.5
MLA_W = MLA_HEADS * MLA_V
FOX_W = FOX_HEADS * FOX_HD
MEM_W = MEM_HEADS * MEM_HD
MIX_W = MLA_W + FOX_W + MEM_W
IN_SIZES = (Q_LORA, KV_LORA, QK_ROPE, FOX_W, FOX_HD, FOX_HD, FOX_HEADS, MEM_W, MIX_W)
RMS_EPS = 1e-6
LN_EPS = 1e-5
PAGE_SIZE = 128
Q_BLOCK = 128
MLA_QK = KV_LORA + QK_ROPE

LANES = 128
SUBLANES = 8
VMEM_LIMIT = 48 * 1024 * 1024
NEG_BIG = -1e30

C_FQ = 0
C_G_MLA = C_FQ + FOX_W
C_G_FOX = C_G_MLA + MLA_W
C_QA = C_G_FOX + FOX_W
C_KVA = C_QA + Q_LORA
C_FK = C_KVA + KV_LORA
C_FV = C_FK + FOX_HD
C_MISC = C_FV + FOX_HD
C_PAD = C_MISC + LANES
C_G_MEM = C_PAD + LANES
C_MQ = C_G_MEM + MEM_W
N_PROJ = C_MQ + MEM_W
MISC_ROPE = LANES - QK_ROPE


def _linear_kernel(x_ref, w_ref, o_ref, acc_ref):
    k = pl.program_id(2)

    @pl.when(k == 0)
    def _():
        acc_ref[...] = jnp.zeros_like(acc_ref)

    acc_ref[...] += jnp.dot(x_ref[...], w_ref[...], preferred_element_type=F32)

    @pl.when(k == pl.num_programs(2) - 1)
    def _():
        o_ref[...] = acc_ref[...].astype(o_ref.dtype)


def _pick(dim, pref):
    if dim <= pref:
        return dim
    t = (pref // LANES) * LANES
    while t > LANES and dim % t:
        t -= LANES
    return t if dim % t == 0 else dim


def _linear(x, w, out_dtype, name):
    m, k = x.shape
    _, n = w.shape
    tm, tn, tk = _pick(m, 1024), _pick(n, 1024), _pick(k, 1024)
    return pl.pallas_call(
        _linear_kernel,
        out_shape=jax.ShapeDtypeStruct((m, n), out_dtype),
        grid=(m // tm, n // tn, k // tk),
        in_specs=[pl.BlockSpec((tm, tk), lambda i, j, kk: (i, kk)),
                  pl.BlockSpec((tk, tn), lambda i, j, kk: (kk, j))],
        out_specs=pl.BlockSpec((tm, tn), lambda i, j, kk: (i, j)),
        scratch_shapes=[pltpu.VMEM((tm, tn), F32)],
        compiler_params=pltpu.CompilerParams(
            dimension_semantics=("parallel", "parallel", "arbitrary"), vmem_limit_bytes=VMEM_LIMIT),
        name=name,
    )(x, w)


def _rope(x, cos, sin):
    half = QK_ROPE // 2
    rot = jnp.concatenate([-x[:, half:], x[:, :half]], axis=-1)
    return x * cos + rot * sin


def _rms(x, g):
    return x * lax.rsqrt(jnp.mean(x * x, axis=-1, keepdims=True) + RMS_EPS) * g


def _prep_kernel(qa_ref, kva_ref, fk_ref, fv_ref, misc_ref, qg_ref, kg_ref, bf_ref, cos_ref, sin_ref,
                 qan_ref, ckv_ref, krope_ref, kmla_ref, fkv_ref, logf_ref):
    qan_ref[...] = _rms(qa_ref[...], qg_ref[...]).astype(BF16)
    ckv = _rms(kva_ref[...], kg_ref[...])
    ckv_ref[...] = ckv
    misc = misc_ref[...]
    kr = _rope(misc[:, MISC_ROPE:], cos_ref[...], sin_ref[...])
    krope_ref[...] = kr
    kmla_ref[:, :KV_LORA] = ckv.astype(BF16)
    kmla_ref[:, KV_LORA:] = kr.astype(BF16)
    fkv_ref[:, :FOX_HD] = fk_ref[...].astype(BF16)
    fkv_ref[:, FOX_HD:] = fv_ref[...].astype(BF16)
    z = misc + bf_ref[...]
    lf = -(jnp.maximum(-z, 0.0) + jnp.log1p(jnp.exp(-jnp.abs(z))))
    lane = lax.broadcasted_iota(jnp.int32, lf.shape, 1)
    logf_ref[...] = jnp.where(lane < FOX_HEADS, lf, 0.0)


def _prep(h, q_norm, kv_norm, b_f, cos, sin):
    m = h.shape[0]
    tm = _pick(m, 512)
    row = lambda blk: (lambda i: (i, blk))
    const = lambda i: (0, 0)
    return pl.pallas_call(
        _prep_kernel,
        out_shape=(jax.ShapeDtypeStruct((m, Q_LORA), BF16),
                   jax.ShapeDtypeStruct((m, KV_LORA), F32),
                   jax.ShapeDtypeStruct((m, QK_ROPE), F32),
                   jax.ShapeDtypeStruct((m, MLA_QK), BF16),
                   jax.ShapeDtypeStruct((m, 2 * FOX_HD), BF16),
                   jax.ShapeDtypeStruct((m, LANES), F32)),
        grid=(m // tm,),
        in_specs=[pl.BlockSpec((tm, Q_LORA), row(C_QA // Q_LORA)),
                  pl.BlockSpec((tm, KV_LORA), row(C_KVA // KV_LORA)),
                  pl.BlockSpec((tm, FOX_HD), row(C_FK // FOX_HD)),
                  pl.BlockSpec((tm, FOX_HD), row(C_FV // FOX_HD)),
                  pl.BlockSpec((tm, LANES), row(C_MISC // LANES)),
                  pl.BlockSpec((1, Q_LORA), const),
                  pl.BlockSpec((1, KV_LORA), const),
                  pl.BlockSpec((1, LANES), const),
                  pl.BlockSpec((tm, QK_ROPE), row(0)),
                  pl.BlockSpec((tm, QK_ROPE), row(0))],
        out_specs=(pl.BlockSpec((tm, Q_LORA), row(0)),
                   pl.BlockSpec((tm, KV_LORA), row(0)),
                   pl.BlockSpec((tm, QK_ROPE), row(0)),
                   pl.BlockSpec((tm, MLA_QK), row(0)),
                   pl.BlockSpec((tm, 2 * FOX_HD), row(0)),
                   pl.BlockSpec((tm, LANES), row(0))),
        compiler_params=pltpu.CompilerParams(dimension_semantics=("parallel",), vmem_limit_bytes=VMEM_LIMIT),
        name="prep",
    )(h, h, h, h, h, q_norm, kv_norm, b_f, cos, sin)


CUM_BLOCK = 256


def _tri_lower(n):
    r = lax.broadcasted_iota(jnp.int32, (n, n), 0)
    c = lax.broadcasted_iota(jnp.int32, (n, n), 1)
    return jnp.where(r >= c, 1.0, 0.0).astype(BF16)


def _block_prefix(tri, x):
    x1 = x.astype(BF16)
    r1 = x - x1.astype(F32)
    x2 = r1.astype(BF16)
    x3 = (r1 - x2.astype(F32)).astype(BF16)
    dot = lambda a: jnp.dot(tri, a, preferred_element_type=F32)
    return dot(x1) + dot(x2) + dot(x3)


def _cumsum_prompt_kernel(lf_ref, c_ref, ct_ref):
    t = lf_ref.shape[0]
    tri = _tri_lower(CUM_BLOCK)
    carry = jnp.zeros((1, LANES), F32)
    for i in range(t // CUM_BLOCK):
        rows = slice(i * CUM_BLOCK, (i + 1) * CUM_BLOCK)
        c = _block_prefix(tri, lf_ref[rows, :]) + carry
        c_ref[rows, :] = c
        ct_ref[0, :, rows] = c.T[:2 * SUBLANES, :]
        carry = c[CUM_BLOCK - 1:, :]


def _cumsum_prompt(logf, batch, seq):
    return pl.pallas_call(
        _cumsum_prompt_kernel,
        out_shape=(jax.ShapeDtypeStruct((batch * seq, LANES), F32),
                   jax.ShapeDtypeStruct((batch, 2 * SUBLANES, seq), F32)),
        grid=(batch,),
        in_specs=[pl.BlockSpec((seq, LANES), lambda b: (b, 0))],
        out_specs=(pl.BlockSpec((seq, LANES), lambda b: (b, 0)),
                   pl.BlockSpec((1, 2 * SUBLANES, seq), lambda b: (b, 0, 0))),
        compiler_params=pltpu.CompilerParams(dimension_semantics=("parallel",), vmem_limit_bytes=VMEM_LIMIT),
        name="cumsum_prompt",
    )(logf)


def _flash_update(s, v, m_ref, l_ref, acc_ref):
    m_prev = m_ref[...]
    m_new = jnp.maximum(m_prev, jnp.max(s, axis=-1, keepdims=True))
    alpha = jnp.exp(m_prev - m_new)
    p = jnp.exp(s - m_new)
    l_ref[...] = alpha * l_ref[...] + jnp.sum(p, axis=-1, keepdims=True)
    acc_ref[...] = alpha * acc_ref[...] + jnp.dot(p.astype(BF16), v, preferred_element_type=F32)
    m_ref[...] = m_new


def _flash_init(m_ref, l_ref, acc_ref):
    m_ref[...] = jnp.full(m_ref.shape, NEG_BIG, F32)
    l_ref[...] = jnp.zeros_like(l_ref)
    acc_ref[...] = jnp.zeros_like(acc_ref)


def _nt_dot(a, b):
    return lax.dot_general(a, b, (((1,), (1,)), ((), ())), preferred_element_type=F32)


def _silu(g):
    return g * (1.0 / (1.0 + jnp.exp(-g)))


def _build_mla_queries(q_ref, cos, sin, wuk_ref, qs_ref, rows):
    for h in range(MLA_HEADS):
        qn = q_ref[:, h * QK_NOPE:(h + 1) * QK_NOPE].astype(BF16)
        ql = jnp.dot(qn, wuk_ref[h], preferred_element_type=F32)
        qs_ref[h * rows:(h + 1) * rows, :KV_LORA] = ql.astype(BF16)
        off = MLA_HEADS * QK_NOPE + h * QK_ROPE
        qs_ref[h * rows:(h + 1) * rows, KV_LORA:] = _rope(q_ref[:, off:off + QK_ROPE], cos, sin).astype(BF16)


def _finish_mla(acc_ref, l_ref, wuv_ref, g_ref, o_ref, rows):
    inv = 1.0 / l_ref[...]
    for h in range(MLA_HEADS):
        sl = slice(h * rows, (h + 1) * rows)
        o_lat = (acc_ref[sl, :] * inv[sl, :]).astype(BF16)
        o = jnp.dot(o_lat, wuv_ref[h], preferred_element_type=F32)
        cols = slice(h * MLA_V, (h + 1) * MLA_V)
        o_ref[:, cols] = (o * _silu(g_ref[:, cols])).astype(o_ref.dtype)


def _finish_fox(acc_ref, l_ref, g_ref, o_ref, rows):
    inv = 1.0 / l_ref[...]
    for h in range(FOX_HEADS):
        sl = slice(h * rows, (h + 1) * rows)
        cols = slice(h * FOX_HD, (h + 1) * FOX_HD)
        o_ref[:, cols] = (acc_ref[sl, :] * inv[sl, :] * _silu(g_ref[:, cols])).astype(o_ref.dtype)


def _causal_mask(s, q0, k0, rows):
    r = lax.broadcasted_iota(jnp.int32, s.shape, 0)
    c = lax.broadcasted_iota(jnp.int32, s.shape, 1)
    return jnp.where(q0 + (r & (rows - 1)) >= k0 + c, s, NEG_BIG)


PROMPT_TK = 256


def _mla_prompt_kernel(q_ref, cos_ref, sin_ref, wuk_ref, k_ref, wuv_ref, g_ref, o_ref,
                       qs_ref, m_ref, l_ref, acc_ref):
    qi = pl.program_id(1)
    tq, tk = Q_BLOCK, PROMPT_TK
    _build_mla_queries(q_ref, cos_ref[...], sin_ref[...], wuk_ref, qs_ref, tq)
    _flash_init(m_ref, l_ref, acc_ref)

    def step(j, masked):
        k = k_ref[pl.ds(pl.multiple_of(j * tk, tk), tk), :]
        s = _nt_dot(qs_ref[...], k) * MLA_SCALE
        if masked:
            s = _causal_mask(s, qi * tq, j * tk, tq)
        _flash_update(s, k[:, :KV_LORA], m_ref, l_ref, acc_ref)

    n_full = (qi * tq + 1) // tk

    def body(j, carry):
        step(j, False)
        return carry

    lax.fori_loop(0, n_full, body, 0)
    step(n_full, True)
    _finish_mla(acc_ref, l_ref, wuv_ref, g_ref, o_ref, tq)


def _fox_prompt_kernel(q_ref, kv_ref, c_ref, ct_ref, g_ref, o_ref, qs_ref, m_ref, l_ref, acc_ref):
    qi = pl.program_id(1)
    tq, tk = Q_BLOCK, PROMPT_TK
    for h in range(FOX_HEADS):
        qs_ref[h * tq:(h + 1) * tq, :] = q_ref[:, h * FOX_HD:(h + 1) * FOX_HD].astype(BF16)
    _flash_init(m_ref, l_ref, acc_ref)
    cq = c_ref[...]

    def step(j, masked):
        k0 = pl.multiple_of(j * tk, tk)
        kv = kv_ref[pl.ds(k0, tk), :]
        s = _nt_dot(qs_ref[...], kv[:, :FOX_HD]) * FOX_SCALE
        ck = ct_ref[0, :, pl.ds(k0, tk)]
        s = jnp.concatenate(
            [s[h * tq:(h + 1) * tq, :] + (cq[:, h:h + 1] - ck[h:h + 1, :]) for h in range(FOX_HEADS)], axis=0)
        if masked:
            s = _causal_mask(s, qi * tq, j * tk, tq)
        _flash_update(s, kv[:, FOX_HD:], m_ref, l_ref, acc_ref)

    n_full = (qi * tq + 1) // tk

    def body(j, carry):
        step(j, False)
        return carry

    lax.fori_loop(0, n_full, body, 0)
    step(n_full, True)
    _finish_fox(acc_ref, l_ref, g_ref, o_ref, tq)


def _mla_prompt(q, cos, sin, wuk, kmla, wuv, h, batch, seq):
    nq = seq // Q_BLOCK
    rows = MLA_HEADS * Q_BLOCK
    tok = lambda blk: (lambda b, i: (b * nq + i, blk))
    return pl.pallas_call(
        _mla_prompt_kernel,
        out_shape=jax.ShapeDtypeStruct((batch * seq, MLA_W), BF16),
        grid=(batch, nq),
        in_specs=[pl.BlockSpec((Q_BLOCK, q.shape[1]), tok(0)),
                  pl.BlockSpec((Q_BLOCK, QK_ROPE), tok(0)),
                  pl.BlockSpec((Q_BLOCK, QK_ROPE), tok(0)),
                  pl.BlockSpec(wuk.shape, lambda b, i: (0, 0, 0)),
                  pl.BlockSpec((seq, MLA_QK), lambda b, i: (b, 0)),
                  pl.BlockSpec(wuv.shape, lambda b, i: (0, 0, 0)),
                  pl.BlockSpec((Q_BLOCK, MLA_W), tok(C_G_MLA // MLA_W))],
        out_specs=pl.BlockSpec((Q_BLOCK, MLA_W), tok(0)),
        scratch_shapes=[pltpu.VMEM((rows, MLA_QK), BF16), pltpu.VMEM((rows, 1), F32),
                        pltpu.VMEM((rows, 1), F32), pltpu.VMEM((rows, KV_LORA), F32)],
        compiler_params=pltpu.CompilerParams(
            dimension_semantics=("parallel", "arbitrary"), vmem_limit_bytes=VMEM_LIMIT),
        name="mla_prompt",
    )(q, cos, sin, wuk, kmla, wuv, h)


def _fox_prompt(h, fkv, c, ct, batch, seq):
    nq = seq // Q_BLOCK
    rows = FOX_HEADS * Q_BLOCK
    tok = lambda blk: (lambda b, i: (b * nq + i, blk))
    return pl.pallas_call(
        _fox_prompt_kernel,
        out_shape=jax.ShapeDtypeStruct((batch * seq, FOX_W), BF16),
        grid=(batch, nq),
        in_specs=[pl.BlockSpec((Q_BLOCK, FOX_W), tok(C_FQ // FOX_W)),
                  pl.BlockSpec((seq, 2 * FOX_HD), lambda b, i: (b, 0)),
                  pl.BlockSpec((Q_BLOCK, LANES), tok(0)),
                  pl.BlockSpec((1, 2 * SUBLANES, seq), lambda b, i: (b, 0, 0)),
                  pl.BlockSpec((Q_BLOCK, FOX_W), tok(C_G_FOX // FOX_W))],
        out_specs=pl.BlockSpec((Q_BLOCK, FOX_W), tok(0)),
        scratch_shapes=[pltpu.VMEM((rows, FOX_HD), BF16), pltpu.VMEM((rows, 1), F32),
                        pltpu.VMEM((rows, 1), F32), pltpu.VMEM((rows, FOX_HD), F32)],
        compiler_params=pltpu.CompilerParams(
            dimension_semantics=("parallel", "arbitrary"), vmem_limit_bytes=VMEM_LIMIT),
        name="fox_prompt",
    )(h, fkv, c, ct, h)


def _mem_attn_kernel(q_ref, k_ref, v_ref, g_ref, o_ref):
    for h in range(MEM_HEADS):
        cols = slice(h * MEM_HD, (h + 1) * MEM_HD)
        s = _nt_dot(q_ref[:, cols].astype(BF16), k_ref[:, cols].astype(BF16)) * MEM_SCALE
        p = jnp.exp(s - jnp.max(s, axis=-1, keepdims=True))
        inv = 1.0 / jnp.sum(p, axis=-1, keepdims=True)
        o = jnp.dot(p.astype(BF16), v_ref[:, cols].astype(BF16), preferred_element_type=F32)
        o_ref[:, cols] = (o * inv * _silu(g_ref[:, cols])).astype(o_ref.dtype)


def _mem_attn(h, k, v, k_blk, v_blk, batch, tq, steps, row0, out_dtype, name):
    slots = k.shape[0] // batch
    r0 = row0 // tq
    tok = lambda blk: (lambda b, i: (r0 + b * steps + i, blk))
    return pl.pallas_call(
        _mem_attn_kernel,
        out_shape=jax.ShapeDtypeStruct((batch * steps * tq, MEM_W), out_dtype),
        grid=(batch, steps),
        in_specs=[pl.BlockSpec((tq, MEM_W), tok(C_MQ // MEM_W)),
                  pl.BlockSpec((slots, MEM_W), lambda b, i: (b, k_blk)),
                  pl.BlockSpec((slots, MEM_W), lambda b, i: (b, v_blk)),
                  pl.BlockSpec((tq, MEM_W), tok(C_G_MEM // MEM_W))],
        out_specs=pl.BlockSpec((tq, MEM_W), lambda b, i: (b * steps + i, 0)),
        compiler_params=pltpu.CompilerParams(
            dimension_semantics=("parallel", "arbitrary"), vmem_limit_bytes=VMEM_LIMIT),
        name=name,
    )(h, k, v, h)


def _fox_cum_sample_kernel(pt_ref, pool_ref, lfn_ref, ct_ref, cq_ref, buf_ref, sem_ref, *, layer, n_pages, dec_seq):
    b = pl.program_id(0)
    nb = pl.num_programs(0)
    slot = b % 2
    past = n_pages * PAGE_SIZE

    def page_copy(bb, i, sl):
        return pltpu.make_async_copy(pool_ref.at[layer, pt_ref[bb, i]],
                                     buf_ref.at[sl, pl.ds(i * PAGE_SIZE, PAGE_SIZE), :],
                                     sem_ref.at[sl])

    def issue(bb, sl):
        for i in range(n_pages):
            page_copy(bb, i, sl).start()

    @pl.when(b == 0)
    def _():
        issue(0, 0)

    @pl.when(b + 1 < nb)
    def _():
        issue(b + 1, 1 - slot)

    for i in range(n_pages):
        page_copy(b, i, slot).wait()

    tri = _tri_lower(CUM_BLOCK)
    carry = jnp.zeros((1, LANES), F32)
    lane_pad = jnp.zeros((CUM_BLOCK, LANES - FOX_HEADS), F32)
    for i in range(past // CUM_BLOCK):
        rows = pl.ds(i * CUM_BLOCK, CUM_BLOCK)
        c = _block_prefix(tri, jnp.concatenate([buf_ref[slot, rows, :], lane_pad], axis=1)) + carry
        ct_ref[0, :, i * CUM_BLOCK:(i + 1) * CUM_BLOCK] = c.T[:2 * SUBLANES, :]
        carry = c[CUM_BLOCK - 1:, :]

    lfn = lfn_ref[...]
    rows_new = []
    for t in range(dec_seq):
        carry = carry + lfn[t:t + 1, :]
        rows_new.append(carry)
    c_new = jnp.concatenate(rows_new + [jnp.zeros((LANES - dec_seq, LANES), F32)], axis=0)
    ct_ref[0, :, past:past + LANES] = c_new.T[:2 * SUBLANES, :]
    cq_ref[0] = jnp.concatenate(
        [jnp.broadcast_to(c_new[:dec_seq, h:h + 1], (dec_seq, LANES)) for h in range(FOX_HEADS)], axis=0)


def _fox_cum_sample(page_table, pool, logf, layer, row0, dec_seq):
    dec_batch, n_pages = page_table.shape
    past = n_pages * PAGE_SIZE
    kern = functools.partial(_fox_cum_sample_kernel, layer=layer, n_pages=n_pages, dec_seq=dec_seq)
    return pl.pallas_call(
        kern,
        out_shape=(jax.ShapeDtypeStruct((dec_batch, 2 * SUBLANES, past + LANES), F32),
                   jax.ShapeDtypeStruct((dec_batch, FOX_HEADS * dec_seq, LANES), F32)),
        grid_spec=pltpu.PrefetchScalarGridSpec(
            num_scalar_prefetch=1,
            grid=(dec_batch,),
            in_specs=[pl.BlockSpec(memory_space=pl.ANY),
                      pl.BlockSpec((dec_seq, LANES), lambda b, pt: (row0 // dec_seq + b, 0))],
            out_specs=(pl.BlockSpec((1, 2 * SUBLANES, past + LANES), lambda b, pt: (b, 0, 0)),
                       pl.BlockSpec((1, FOX_HEADS * dec_seq, LANES), lambda b, pt: (b, 0, 0))),
            scratch_shapes=[pltpu.VMEM((2, past, FOX_HEADS), F32), pltpu.SemaphoreType.DMA((2,))]),
        compiler_params=pltpu.CompilerParams(dimension_semantics=("arbitrary",), vmem_limit_bytes=VMEM_LIMIT),
        name="fox_cum_sample",
    )(page_table, pool, logf)


PAGES_PER_CHUNK = 16


def _sample_attn_kernel(pt_ref, q_ref, cos_ref, sin_ref, wuk_ref, wuv_ref, fq_ref,
                        ckvn_ref, krn_ref, fkn_ref, fvn_ref, ct_ref, ctn_ref, cq_ref, gm_ref, gf_ref,
                        ckv_pool, kr_pool, fk_pool, fv_pool,
                        om_ref, of_ref,
                        ckv_buf, kr_buf, fk_buf, fv_buf, sem_ref,
                        qm_ref, qf_ref, m1_ref, l1_ref, a1_ref, m2_ref, l2_ref, a2_ref,
                        *, layer, n_chunks, ppc, dec_seq):
    b = pl.program_id(0)
    j = pl.program_id(1)
    g = b * n_chunks + j
    total = pl.num_programs(0) * n_chunks
    slot = g % 2
    pools = (ckv_pool, kr_pool, fk_pool, fv_pool)
    bufs = (ckv_buf, kr_buf, fk_buf, fv_buf)

    def page_copy(a, bb, jj, i, sl):
        return pltpu.make_async_copy(pools[a].at[layer, pt_ref[bb, jj * ppc + i]],
                                     bufs[a].at[sl, pl.ds(i * PAGE_SIZE, PAGE_SIZE), :],
                                     sem_ref.at[a, sl])

    def issue(bb, jj, sl):
        for i in range(ppc):
            for a in range(4):
                page_copy(a, bb, jj, i, sl).start()

    @pl.when(g == 0)
    def _():
        issue(0, 0, 0)

    @pl.when(g + 1 < total)
    def _():
        nxt = g + 1
        issue(nxt // n_chunks, nxt % n_chunks, 1 - slot)

    @pl.when(j == 0)
    def _():
        _build_mla_queries(q_ref, cos_ref[...], sin_ref[...], wuk_ref, qm_ref, dec_seq)
        for h in range(FOX_HEADS):
            qf_ref[h * dec_seq:(h + 1) * dec_seq, :] = fq_ref[:, h * FOX_HD:(h + 1) * FOX_HD].astype(BF16)
        _flash_init(m1_ref, l1_ref, a1_ref)
        _flash_init(m2_ref, l2_ref, a2_ref)

    for i in range(ppc):
        for a in range(4):
            page_copy(a, b, j, i, slot).wait()

    def fox_bias(ck):
        ck_rows = jnp.concatenate(
            [jnp.broadcast_to(ck[h:h + 1, :], (dec_seq, ck.shape[1])) for h in range(FOX_HEADS)], axis=0)
        return cq_ref[0][:, :1] - ck_rows

    ckv = ckv_buf[slot].astype(BF16)
    kr = kr_buf[slot].astype(BF16)
    qm = qm_ref[...]
    s1 = (_nt_dot(qm[:, :KV_LORA], ckv) + _nt_dot(qm[:, KV_LORA:], kr)) * MLA_SCALE
    _flash_update(s1, ckv, m1_ref, l1_ref, a1_ref)
    fk = fk_buf[slot].astype(BF16)
    s2 = _nt_dot(qf_ref[...], fk) * FOX_SCALE + fox_bias(ct_ref[0])
    _flash_update(s2, fv_buf[slot].astype(BF16), m2_ref, l2_ref, a2_ref)

    @pl.when(j == n_chunks - 1)
    def _():
        pad = lambda x: jnp.concatenate(
            [x, jnp.zeros((LANES - dec_seq, x.shape[1]), x.dtype)], axis=0).astype(BF16)
        rows = FOX_HEADS * dec_seq
        r = lax.broadcasted_iota(jnp.int32, (rows, LANES), 0)
        c = lax.broadcasted_iota(jnp.int32, (rows, LANES), 1)
        visible = c <= (r & (dec_seq - 1))
        ckv_n = pad(ckvn_ref[...])
        qmm = qm_ref[...]
        s1n = (_nt_dot(qmm[:, :KV_LORA], ckv_n) + _nt_dot(qmm[:, KV_LORA:], pad(krn_ref[...]))) * MLA_SCALE
        _flash_update(jnp.where(visible, s1n, NEG_BIG), ckv_n, m1_ref, l1_ref, a1_ref)
        s2n = _nt_dot(qf_ref[...], pad(fkn_ref[...])) * FOX_SCALE + fox_bias(ctn_ref[0])
        _flash_update(jnp.where(visible, s2n, NEG_BIG), pad(fvn_ref[...]), m2_ref, l2_ref, a2_ref)
        _finish_mla(a1_ref, l1_ref, wuv_ref, gm_ref, om_ref, dec_seq)
        _finish_fox(a2_ref, l2_ref, gf_ref, of_ref, dec_seq)


def _sample_attn(page_table, q, cos, sin, wuk, wuv, h, ckv, krope, ct, cq,
                 ckv_pool, kr_pool, fk_pool, fv_pool, layer, row0, dec_seq):
    dec_batch, n_pages = page_table.shape
    ppc = PAGES_PER_CHUNK if n_pages % PAGES_PER_CHUNK == 0 else n_pages
    n_chunks = n_pages // ppc
    tkc = ppc * PAGE_SIZE
    rows = MLA_HEADS * dec_seq
    r0 = row0 // dec_seq
    tok = lambda blk: (lambda b, j, pt: (r0 + b, blk))
    const3 = lambda b, j, pt: (0, 0, 0)
    anyspec = pl.BlockSpec(memory_space=pl.ANY)
    kern = functools.partial(_sample_attn_kernel, layer=layer, n_chunks=n_chunks, ppc=ppc, dec_seq=dec_seq)
    return pl.pallas_call(
        kern,
        out_shape=(jax.ShapeDtypeStruct((dec_batch * dec_seq, MLA_W), F32),
                   jax.ShapeDtypeStruct((dec_batch * dec_seq, FOX_W), F32)),
        grid_spec=pltpu.PrefetchScalarGridSpec(
            num_scalar_prefetch=1,
            grid=(dec_batch, n_chunks),
            in_specs=[pl.BlockSpec((dec_seq, q.shape[1]), tok(0)),
                      pl.BlockSpec((dec_seq, QK_ROPE), tok(0)),
                      pl.BlockSpec((dec_seq, QK_ROPE), tok(0)),
                      pl.BlockSpec(wuk.shape, const3),
                      pl.BlockSpec(wuv.shape, const3),
                      pl.BlockSpec((dec_seq, FOX_W), tok(C_FQ // FOX_W)),
                      pl.BlockSpec((dec_seq, KV_LORA), tok(0)),
                      pl.BlockSpec((dec_seq, QK_ROPE), tok(0)),
                      pl.BlockSpec((dec_seq, FOX_HD), tok(C_FK // FOX_HD)),
                      pl.BlockSpec((dec_seq, FOX_HD), tok(C_FV // FOX_HD)),
                      pl.BlockSpec((1, 2 * SUBLANES, tkc), lambda b, j, pt: (b, 0, j)),
                      pl.BlockSpec((1, 2 * SUBLANES, LANES), lambda b, j, pt: (b, 0, n_pages)),
                      pl.BlockSpec((1, rows, LANES), lambda b, j, pt: (b, 0, 0)),
                      pl.BlockSpec((dec_seq, MLA_W), tok(C_G_MLA // MLA_W)),
                      pl.BlockSpec((dec_seq, FOX_W), tok(C_G_FOX // FOX_W)),
                      anyspec, anyspec, anyspec, anyspec],
            out_specs=(pl.BlockSpec((dec_seq, MLA_W), lambda b, j, pt: (b, 0)),
                       pl.BlockSpec((dec_seq, FOX_W), lambda b, j, pt: (b, 0))),
            scratch_shapes=[pltpu.VMEM((2, tkc, KV_LORA), F32), pltpu.VMEM((2, tkc, QK_ROPE), F32),
                            pltpu.VMEM((2, tkc, FOX_HD), F32), pltpu.VMEM((2, tkc, FOX_HD), F32),
                            pltpu.SemaphoreType.DMA((4, 2)),
                            pltpu.VMEM((rows, MLA_QK), BF16), pltpu.VMEM((rows, FOX_HD), BF16),
                            pltpu.VMEM((rows, 1), F32), pltpu.VMEM((rows, 1), F32), pltpu.VMEM((rows, KV_LORA), F32),
                            pltpu.VMEM((rows, 1), F32), pltpu.VMEM((rows, 1), F32), pltpu.VMEM((rows, FOX_HD), F32)]),
        compiler_params=pltpu.CompilerParams(
            dimension_semantics=("arbitrary", "arbitrary"), vmem_limit_bytes=VMEM_LIMIT),
        name="sample_attn",
    )(page_table, q, cos, sin, wuk, wuv, h, ckv, krope, h, h, ct, ct, cq, h, h,
      ckv_pool, kr_pool, fk_pool, fv_pool)


def _ln_kernel(x_ref, y_ref, g_ref, b_ref, o_ref, ob_ref, *, alpha):
    v = alpha * x_ref[...] + y_ref[...]
    mu = jnp.mean(v, axis=-1, keepdims=True)
    vc = v - mu
    var = jnp.mean(vc * vc, axis=-1, keepdims=True)
    out = vc * lax.rsqrt(var + LN_EPS) * g_ref[...] + b_ref[...]
    o_ref[...] = out
    ob_ref[...] = out.astype(BF16)


def _layernorm(x, y, g, b, alpha):
    m, d = x.shape
    tm = _pick(m, 256)
    row = lambda i: (i, 0)
    const = lambda i: (0, 0)
    return pl.pallas_call(
        functools.partial(_ln_kernel, alpha=alpha),
        out_shape=(jax.ShapeDtypeStruct((m, d), F32), jax.ShapeDtypeStruct((m, d), BF16)),
        grid=(m // tm,),
        in_specs=[pl.BlockSpec((tm, d), row), pl.BlockSpec((tm, d), row),
                  pl.BlockSpec((1, d), const), pl.BlockSpec((1, d), const)],
        out_specs=(pl.BlockSpec((tm, d), row), pl.BlockSpec((tm, d), row)),
        compiler_params=pltpu.CompilerParams(dimension_semantics=("parallel",), vmem_limit_bytes=VMEM_LIMIT),
        name="residual_layernorm",
    )(x, y, g, b)


def _split_cols(w):
    offs = [0]
    for s in IN_SIZES:
        offs.append(offs[-1] + s)
    return [w[..., offs[i]:offs[i + 1]] for i in range(len(IN_SIZES))]


def _arrange_w_in(w_in):
    q_a, kv_a, k_pe, fq, fk, fv, f_logit, mq, gate = _split_cols(w_in)
    lead = w_in.shape[:-1]
    zeros = lambda n: jnp.zeros(lead + (n,), w_in.dtype)
    misc = jnp.concatenate([f_logit, zeros(LANES - FOX_HEADS - QK_ROPE), k_pe], axis=-1)
    parts = [fq, gate[..., :MLA_W], gate[..., MLA_W:MLA_W + FOX_W], q_a, kv_a, fk, fv, misc, zeros(LANES),
             gate[..., MLA_W + FOX_W:], mq]
    out = jnp.concatenate(parts, axis=-1).astype(BF16)
    assert out.shape[-1] == N_PROJ
    return out


def _arrange_w_q_up(w_q_up):
    depth, lora, _ = w_q_up.shape
    w = w_q_up.reshape(depth, lora, MLA_HEADS, QK_NOPE + QK_ROPE)
    nope = w[..., :QK_NOPE].reshape(depth, lora, MLA_HEADS * QK_NOPE)
    rope = w[..., QK_NOPE:].reshape(depth, lora, MLA_HEADS * QK_ROPE)
    return jnp.concatenate([nope, rope], axis=-1).astype(BF16)


def _rope_tables(positions):
    half = QK_ROPE // 2
    inv = ROPE_THETA ** (-jnp.arange(half, dtype=F32) / half)
    ang = positions.astype(F32)[:, None] * inv[None, :]
    cos, sin = jnp.cos(ang), jnp.sin(ang)
    return jnp.concatenate([cos, cos], axis=-1), jnp.concatenate([sin, sin], axis=-1)


def kernel(x_prompt, x_sample, cache_mla_ckv, cache_mla_krope, cache_fox_k, cache_fox_v, cache_fox_logf,
           cache_mem_k, cache_mem_v, page_table, mem_prompt, w_in, b_f, q_a_norm, w_q_up, kv_a_norm, w_kv_up,
           w_mem_k, w_mem_v, w_out, ln_g, ln_b):
    batch, seq, d_model = x_prompt.shape
    dec_batch, dec_seq, _ = x_sample.shape
    depth = w_in.shape[0]
    n_pages = page_table.shape[1]
    past = n_pages * PAGE_SIZE
    mem_slots = mem_prompt.shape[1]
    assert w_in.shape[-1] == sum(IN_SIZES) and cache_mla_ckv.shape[2] == PAGE_SIZE
    assert seq % PROMPT_TK == 0 and PROMPT_TK % Q_BLOCK == 0 and dec_seq % SUBLANES == 0
    alpha = (2 * depth) ** 0.25
    n_p = batch * seq
    n_s = dec_batch * dec_seq

    w_in_r = _arrange_w_in(w_in)
    w_q_r = _arrange_w_q_up(w_q_up)
    wuk = jnp.transpose(w_kv_up[..., :QK_NOPE], (0, 2, 3, 1)).astype(BF16)
    wuv = jnp.transpose(w_kv_up[..., QK_NOPE:], (0, 2, 1, 3)).astype(BF16)
    w_mem = jnp.concatenate([w_mem_k, w_mem_v], axis=-1).astype(BF16)
    w_out_b = w_out.astype(BF16)
    b_f_pad = jnp.pad(b_f, ((0, 0), (0, LANES - FOX_HEADS)))
    mem_b = mem_prompt.reshape(batch * mem_slots, d_model).astype(BF16)
    mem_k_s = cache_mem_k.reshape(depth, dec_batch * mem_slots, MEM_W)
    mem_v_s = cache_mem_v.reshape(depth, dec_batch * mem_slots, MEM_W)

    pos = jnp.concatenate([jnp.tile(jnp.arange(seq), batch), jnp.tile(past + jnp.arange(dec_seq), dec_batch)])
    cos, sin = _rope_tables(pos)

    x = jnp.concatenate([x_prompt.reshape(n_p, d_model), x_sample.reshape(n_s, d_model)], axis=0)
    xb = x.astype(BF16)

    outs = {k: [] for k in ("p_ckv", "p_kr", "p_fk", "p_fv", "p_lf", "p_mk", "p_mv",
                            "s_ckv", "s_kr", "s_fk", "s_fv", "s_lf")}
    for l in range(depth):
        h = _linear(xb, w_in_r[l], F32, "proj_in")
        qan, ckv, krope, kmla, fkv, logf = _prep(h, q_a_norm[l][None], kv_a_norm[l][None], b_f_pad[l][None], cos, sin)
        q = _linear(qan, w_q_r[l], F32, "q_up")

        c, ct = _cumsum_prompt(logf, batch, seq)
        o_mla_p = _mla_prompt(q, cos, sin, wuk[l], kmla, wuv[l], h, batch, seq)
        o_fox_p = _fox_prompt(h, fkv, c, ct, batch, seq)
        mkv = _linear(mem_b, w_mem[l], F32, "mem_kv")
        o_mem_p = _mem_attn(h, mkv, mkv, 0, 1, batch, _pick(seq, 512), seq // _pick(seq, 512), 0, BF16,
                            "mem_attn_prompt")

        ct_s, cq_s = _fox_cum_sample(page_table, cache_fox_logf, logf, l, n_p, dec_seq)
        o_mla_s, o_fox_s = _sample_attn(page_table, q, cos, sin, wuk[l], wuv[l], h, ckv, krope, ct_s, cq_s,
                                        cache_mla_ckv, cache_mla_krope, cache_fox_k, cache_fox_v, l, n_p, dec_seq)
        o_mem_s = _mem_attn(h, mem_k_s[l], mem_v_s[l], 0, 0, dec_batch, dec_seq, 1, n_p, F32, "mem_attn_sample")

        a = jnp.concatenate(
            [jnp.concatenate([o_mla_p, o_fox_p, o_mem_p], axis=1),
             jnp.concatenate([o_mla_s, o_fox_s, o_mem_s], axis=1).astype(BF16)], axis=0)
        y = _linear(a, w_out_b[l], F32, "proj_out")
        x, xb = _layernorm(x, y, ln_g[l][None], ln_b[l][None], alpha)

        fk, fv = h[:, C_FK:C_FK + FOX_HD], h[:, C_FV:C_FV + FOX_HD]
        lf = logf[:, :FOX_HEADS]
        outs["p_ckv"].append(ckv[:n_p].reshape(batch, seq, KV_LORA))
        outs["p_kr"].append(krope[:n_p].reshape(batch, seq, QK_ROPE))
        outs["p_fk"].append(fk[:n_p].reshape(batch, seq, FOX_HD))
        outs["p_fv"].append(fv[:n_p].reshape(batch, seq, FOX_HD))
        outs["p_lf"].append(lf[:n_p].reshape(batch, seq, FOX_HEADS))
        outs["p_mk"].append(mkv[:, :MEM_W].reshape(batch, mem_slots, MEM_HEADS, MEM_HD))
        outs["p_mv"].append(mkv[:, MEM_W:].reshape(batch, mem_slots, MEM_HEADS, MEM_HD))
        outs["s_ckv"].append(ckv[n_p:].reshape(dec_batch, dec_seq, KV_LORA))
        outs["s_kr"].append(krope[n_p:].reshape(dec_batch, dec_seq, QK_ROPE))
        outs["s_fk"].append(fk[n_p:].reshape(dec_batch, dec_seq, FOX_HD))
        outs["s_fv"].append(fv[n_p:].reshape(dec_batch, dec_seq, FOX_HD))
        outs["s_lf"].append(lf[n_p:].reshape(dec_batch, dec_seq, FOX_HEADS))

    st = {k: jnp.stack(v) for k, v in outs.items()}
    return (x[:n_p].reshape(batch, seq, d_model), x[n_p:].reshape(dec_batch, dec_seq, d_model),
            st["p_ckv"], st["p_kr"], st["p_fk"], st["p_fv"], st["p_lf"], st["p_mk"], st["p_mv"],
            st["s_ckv"], st["s_kr"], st["s_fk"], st["s_fv"], st["s_lf"])
```

```python
import functools
import math

import jax
import jax.numpy as jnp
from jax import lax
from jax.experimental import pallas as pl
from jax.experimental.pallas import tpu as pltpu

F32 = jnp.float32
BF16 = jnp.bfloat16

MLA_HEADS = 12
Q_LORA = 768
KV_LORA = 256
QK_NOPE = 128
QK_ROPE = 64
MLA_V = 128
ROPE_THETA = 10000.0
MLA_SCALE = (QK_NOPE + QK_ROPE) ** -0.5
FOX_HEADS = 12
FOX_HD = 128
FOX_SCALE = FOX_HD ** -0.5
MEM_HEADS = 4
MEM_HD = 256
MEM_SCALE = MEM_HD ** -0.5
MLA_W = MLA_HEADS * MLA_V
FOX_W = FOX_HEADS * FOX_HD
MEM_W = MEM_HEADS * MEM_HD
MIX_W = MLA_W + FOX_W + MEM_W
IN_SIZES = (Q_LORA, KV_LORA, QK_ROPE, FOX_W, FOX_HD, FOX_HD, FOX_HEADS, MEM_W, MIX_W)
RMS_EPS = 1e-6
LN_EPS = 1e-5
PAGE_SIZE = 128
Q_BLOCK = 128
MLA_QK = KV_LORA + QK_ROPE

LANES = 128
SUBLANES = 8
HEAD_ROWS = 2 * SUBLANES
VMEM_LIMIT = 48 * 1024 * 1024
WEIGHT_TILE_BYTES = 4 * 1024 * 1024
NEG_BIG = -1e30

C_FQ = 0
C_G_MLA = C_FQ + FOX_W
C_G_FOX = C_G_MLA + MLA_W
C_QA = C_G_FOX + FOX_W
C_KVA = C_QA + Q_LORA
C_FK = C_KVA + KV_LORA
C_FV = C_FK + FOX_HD
C_MISC = C_FV + FOX_HD
C_PAD = C_MISC + LANES
C_G_MEM = C_PAD + LANES
C_MQ = C_G_MEM + MEM_W
N_PROJ = C_MQ + MEM_W
MISC_ROPE = LANES - QK_ROPE


def _pick(dim, pref):
    if dim <= pref:
        return dim
    t = (pref // LANES) * LANES
    while t > LANES and dim % t:
        t -= LANES
    return t if dim % t == 0 else dim


def _params(*semantics):
    return pltpu.CompilerParams(dimension_semantics=semantics, vmem_limit_bytes=VMEM_LIMIT)


def _linear_kernel(x_ref, w_ref, o_ref):
    o_ref[...] = jnp.dot(x_ref[...], w_ref[...], preferred_element_type=F32).astype(o_ref.dtype)


def _linear(x, w, layer, out_dtype, name):
    m, k = x.shape
    n = w.shape[-1]
    tm = _pick(m, 1024)
    tn = _pick(n, max(LANES, min(1024, WEIGHT_TILE_BYTES // (2 * k))))
    return pl.pallas_call(
        _linear_kernel,
        out_shape=jax.ShapeDtypeStruct((m, n), out_dtype),
        grid=(m // tm, n // tn),
        in_specs=[pl.BlockSpec((tm, k), lambda i, j: (i, 0)),
                  pl.BlockSpec((None, k, tn), lambda i, j: (layer, 0, j))],
        out_specs=pl.BlockSpec((tm, tn), lambda i, j: (i, j)),
        compiler_params=_params("parallel", "arbitrary"),
        name=name,
    )(x, w)


def _proj_out_kernel(p1_ref, p2_ref, p3_ref, s1_ref, s2_ref, s3_ref, w_ref, o_ref, *, n_prompt_tiles):
    def run(r1, r2, r3):
        dot = lambda a, lo, hi: jnp.dot(a[...], w_ref[lo:hi, :], preferred_element_type=F32)
        o_ref[...] = dot(r1, 0, MLA_W) + dot(r2, MLA_W, MLA_W + FOX_W) + dot(r3, MLA_W + FOX_W, MIX_W)

    i = pl.program_id(0)

    @pl.when(i < n_prompt_tiles)
    def _():
        run(p1_ref, p2_ref, p3_ref)

    @pl.when(i >= n_prompt_tiles)
    def _():
        run(s1_ref, s2_ref, s3_ref)


def _proj_out(o_mla_p, o_fox_p, o_mem_p, a_s, w_out, layer):
    n_p, n_s = o_mla_p.shape[0], a_s.shape[0]
    d = w_out.shape[-1]
    tm = _pick(math.gcd(n_p, n_s), 512)
    tn = _pick(d, 512)
    npt = n_p // tm
    prow = lambda i, j: (jnp.minimum(i, npt - 1), 0)
    srow = lambda blk: (lambda i, j: (jnp.maximum(i - npt, 0), blk))
    return pl.pallas_call(
        functools.partial(_proj_out_kernel, n_prompt_tiles=npt),
        out_shape=jax.ShapeDtypeStruct((n_p + n_s, d), F32),
        grid=((n_p + n_s) // tm, d // tn),
        in_specs=[pl.BlockSpec((tm, MLA_W), prow), pl.BlockSpec((tm, FOX_W), prow), pl.BlockSpec((tm, MEM_W), prow),
                  pl.BlockSpec((tm, MLA_W), srow(0)), pl.BlockSpec((tm, FOX_W), srow(MLA_W // FOX_W)),
                  pl.BlockSpec((tm, MEM_W), srow((MLA_W + FOX_W) // MEM_W)),
                  pl.BlockSpec((None, MIX_W, tn), lambda i, j: (layer, 0, j))],
        out_specs=pl.BlockSpec((tm, tn), lambda i, j: (i, j)),
        compiler_params=_params("parallel", "arbitrary"),
        name="proj_out",
    )(o_mla_p, o_fox_p, o_mem_p, a_s, a_s, a_s, w_out)


def _rope(x, cos, sin):
    half = QK_ROPE // 2
    rot = jnp.concatenate([-x[:, half:], x[:, :half]], axis=-1)
    return x * cos + rot * sin


def _rms(x, g):
    return x * lax.rsqrt(jnp.mean(x * x, axis=-1, keepdims=True) + RMS_EPS) * g


def _prep_kernel(qa_ref, kva_ref, fk_ref, fv_ref, misc_ref, qg_ref, kg_ref, bf_ref, cos_ref, sin_ref,
                 qan_ref, ckv_ref, krope_ref, kmla_ref, fkv_ref, logf_ref):
    qan_ref[...] = _rms(qa_ref[...], qg_ref[...]).astype(BF16)
    ckv = _rms(kva_ref[...], kg_ref[...])
    ckv_ref[...] = ckv
    misc = misc_ref[...]
    kr = _rope(misc[:, MISC_ROPE:], cos_ref[...], sin_ref[...])
    krope_ref[...] = kr
    kmla_ref[:, :KV_LORA] = ckv.astype(BF16)
    kmla_ref[:, KV_LORA:] = kr.astype(BF16)
    fkv_ref[:, :FOX_HD] = fk_ref[...].astype(BF16)
    fkv_ref[:, FOX_HD:] = fv_ref[...].astype(BF16)
    z = misc + bf_ref[...]
    lf = -(jnp.maximum(-z, 0.0) + jnp.log1p(jnp.exp(-jnp.abs(z))))
    lane = lax.broadcasted_iota(jnp.int32, lf.shape, 1)
    logf_ref[...] = jnp.where(lane < FOX_HEADS, lf, 0.0)


def _prep(h, q_norm, kv_norm, b_f, cos, sin, layer):
    m = h.shape[0]
    tm = _pick(m, 512)
    row = lambda blk: (lambda i: (i, blk))
    par = lambda i: (layer, 0, 0)
    return pl.pallas_call(
        _prep_kernel,
        out_shape=(jax.ShapeDtypeStruct((m, Q_LORA), BF16),
                   jax.ShapeDtypeStruct((m, KV_LORA), F32),
                   jax.ShapeDtypeStruct((m, QK_ROPE), F32),
                   jax.ShapeDtypeStruct((m, MLA_QK), BF16),
                   jax.ShapeDtypeStruct((m, 2 * FOX_HD), BF16),
                   jax.ShapeDtypeStruct((m, LANES), F32)),
        grid=(m // tm,),
        in_specs=[pl.BlockSpec((tm, Q_LORA), row(C_QA // Q_LORA)),
                  pl.BlockSpec((tm, KV_LORA), row(C_KVA // KV_LORA)),
                  pl.BlockSpec((tm, FOX_HD), row(C_FK // FOX_HD)),
                  pl.BlockSpec((tm, FOX_HD), row(C_FV // FOX_HD)),
                  pl.BlockSpec((tm, LANES), row(C_MISC // LANES)),
                  pl.BlockSpec((None, 1, Q_LORA), par),
                  pl.BlockSpec((None, 1, KV_LORA), par),
                  pl.BlockSpec((None, 1, LANES), par),
                  pl.BlockSpec((tm, QK_ROPE), row(0)),
                  pl.BlockSpec((tm, QK_ROPE), row(0))],
        out_specs=(pl.BlockSpec((tm, Q_LORA), row(0)),
                   pl.BlockSpec((tm, KV_LORA), row(0)),
                   pl.BlockSpec((tm, QK_ROPE), row(0)),
                   pl.BlockSpec((tm, MLA_QK), row(0)),
                   pl.BlockSpec((tm, 2 * FOX_HD), row(0)),
                   pl.BlockSpec((tm, LANES), row(0))),
        compiler_params=_params("parallel"),
        name="prep",
    )(h, h, h, h, h, q_norm, kv_norm, b_f, cos, sin)


CUM_BLOCK = 256


def _tri(n, upper):
    r = lax.broadcasted_iota(jnp.int32, (n, n), 0)
    c = lax.broadcasted_iota(jnp.int32, (n, n), 1)
    return jnp.where((r <= c) if upper else (r >= c), 1.0, 0.0).astype(BF16)


def _split3(x):
    x1 = x.astype(BF16)
    r1 = x - x1.astype(F32)
    x2 = r1.astype(BF16)
    x3 = (r1 - x2.astype(F32)).astype(BF16)
    return x1, x2, x3


def _prefix_rows(tri_lower, x):
    return sum(jnp.dot(tri_lower, t, preferred_element_type=F32) for t in _split3(x))


def _prefix_lanes(x, tri_upper):
    return sum(jnp.dot(t, tri_upper, preferred_element_type=F32) for t in _split3(x))


def _cumsum_prompt_kernel(lf_ref, c_ref, ct_ref):
    t = lf_ref.shape[0]
    tri = _tri(CUM_BLOCK, upper=False)
    carry = jnp.zeros((1, LANES), F32)
    for i in range(t // CUM_BLOCK):
        rows = slice(i * CUM_BLOCK, (i + 1) * CUM_BLOCK)
        c = _prefix_rows(tri, lf_ref[rows, :]) + carry
        c_ref[rows, :] = c
        ct_ref[0, :, rows] = c.T[:HEAD_ROWS, :]
        carry = c[CUM_BLOCK - 1:, :]


def _cumsum_prompt(logf, batch, seq):
    return pl.pallas_call(
        _cumsum_prompt_kernel,
        out_shape=(jax.ShapeDtypeStruct((batch * seq, LANES), F32),
                   jax.ShapeDtypeStruct((batch, HEAD_ROWS, seq), F32)),
        grid=(batch,),
        in_specs=[pl.BlockSpec((seq, LANES), lambda b: (b, 0))],
        out_specs=(pl.BlockSpec((seq, LANES), lambda b: (b, 0)),
                   pl.BlockSpec((1, HEAD_ROWS, seq), lambda b: (b, 0, 0))),
        compiler_params=_params("parallel"),
        name="cumsum_prompt",
    )(logf)


def _lane_chunks(s):
    return [s[:, c * LANES:(c + 1) * LANES] for c in range(s.shape[1] // LANES)]


def _flash_update(chunks, v, m_ref, l_ref, acc_ref):
    m_prev = m_ref[...]
    cmax = functools.reduce(jnp.maximum, chunks)
    m_new = jnp.maximum(m_prev, jnp.max(cmax, axis=-1, keepdims=True))
    alpha = jnp.exp(m_prev - m_new)
    ps = [jnp.exp(c - m_new) for c in chunks]
    l_ref[...] = alpha * l_ref[...] + functools.reduce(jnp.add, ps)
    p = jnp.concatenate([x.astype(BF16) for x in ps], axis=1)
    pv = jnp.dot(p, v, preferred_element_type=F32)
    for c in range(acc_ref.shape[1] // LANES):
        cols = slice(c * LANES, (c + 1) * LANES)
        acc_ref[:, cols] = alpha * acc_ref[:, cols] + pv[:, cols]
    m_ref[...] = m_new


def _flash_init(m_ref, l_ref, acc_ref):
    m_ref[...] = jnp.full(m_ref.shape, NEG_BIG, F32)
    l_ref[...] = jnp.zeros_like(l_ref)
    acc_ref[...] = jnp.zeros_like(acc_ref)


def _row_scale(l_ref):
    return 1.0 / jnp.sum(l_ref[...], axis=-1, keepdims=True)


def _nt_dot(a, b):
    return lax.dot_general(a, b, (((1,), (1,)), ((), ())), preferred_element_type=F32)


def _silu(g):
    return g * (1.0 / (1.0 + jnp.exp(-g)))


def _build_mla_queries(q_ref, cos, sin, wuk_ref, qs_ref, rows):
    for h in range(MLA_HEADS):
        qn = q_ref[:, h * QK_NOPE:(h + 1) * QK_NOPE].astype(BF16)
        ql = jnp.dot(qn, wuk_ref[h], preferred_element_type=F32)
        qs_ref[h * rows:(h + 1) * rows, :KV_LORA] = ql.astype(BF16)
        off = MLA_HEADS * QK_NOPE + h * QK_ROPE
        qs_ref[h * rows:(h + 1) * rows, KV_LORA:] = _rope(q_ref[:, off:off + QK_ROPE], cos, sin).astype(BF16)


def _finish_mla(acc_ref, l_ref, wuv_ref, g_ref, o_ref, rows):
    inv = _row_scale(l_ref)
    for h in range(MLA_HEADS):
        sl = slice(h * rows, (h + 1) * rows)
        o_lat = (acc_ref[sl, :] * inv[sl, :]).astype(BF16)
        o = jnp.dot(o_lat, wuv_ref[h], preferred_element_type=F32)
        cols = slice(h * MLA_V, (h + 1) * MLA_V)
        o_ref[:, cols] = (o * _silu(g_ref[:, cols])).astype(o_ref.dtype)


def _finish_fox(acc_ref, l_ref, g_ref, o_ref, rows):
    inv = _row_scale(l_ref)
    for h in range(FOX_HEADS):
        sl = slice(h * rows, (h + 1) * rows)
        cols = slice(h * FOX_HD, (h + 1) * FOX_HD)
        o_ref[:, cols] = (acc_ref[sl, :] * inv[sl, :] * _silu(g_ref[:, cols])).astype(o_ref.dtype)


def _causal_mask(s, q0, k0, rows):
    r = lax.broadcasted_iota(jnp.int32, s.shape, 0)
    c = lax.broadcasted_iota(jnp.int32, s.shape, 1)
    return jnp.where(q0 + (r & (rows - 1)) >= k0 + c, s, NEG_BIG)


def _head_rows(x, rows):
    return jnp.concatenate([jnp.broadcast_to(x[h:h + 1, :], (rows, x.shape[1])) for h in range(FOX_HEADS)], axis=0)


PROMPT_TK = 256


def _mla_prompt_kernel(q_ref, cos_ref, sin_ref, wuk_ref, k_ref, wuv_ref, g_ref, o_ref,
                       qs_ref, m_ref, l_ref, acc_ref):
    qi = pl.program_id(1)
    tq, tk = Q_BLOCK, PROMPT_TK
    _build_mla_queries(q_ref, cos_ref[...], sin_ref[...], wuk_ref, qs_ref, tq)
    _flash_init(m_ref, l_ref, acc_ref)

    def step(j, masked):
        k = k_ref[pl.ds(pl.multiple_of(j * tk, tk), tk), :]
        s = _nt_dot(qs_ref[...], k) * MLA_SCALE
        if masked:
            s = _causal_mask(s, qi * tq, j * tk, tq)
        _flash_update(_lane_chunks(s), k[:, :KV_LORA], m_ref, l_ref, acc_ref)

    n_full = (qi * tq + 1) // tk

    def body(j, carry):
        step(j, False)
        return carry

    lax.fori_loop(0, n_full, body, 0)
    step(n_full, True)
    _finish_mla(acc_ref, l_ref, wuv_ref, g_ref, o_ref, tq)


def _fox_prompt_kernel(q_ref, kv_ref, c_ref, ct_ref, g_ref, o_ref, qs_ref, cq_ref, m_ref, l_ref, acc_ref):
    qi = pl.program_id(1)
    tq, tk = Q_BLOCK, PROMPT_TK
    for h in range(FOX_HEADS):
        rows = slice(h * tq, (h + 1) * tq)
        qs_ref[rows, :] = q_ref[:, h * FOX_HD:(h + 1) * FOX_HD].astype(BF16)
        cq_ref[rows, :] = jnp.broadcast_to(c_ref[:, h:h + 1], (tq, LANES))
    _flash_init(m_ref, l_ref, acc_ref)

    def step(j, masked):
        k0 = pl.multiple_of(j * tk, tk)
        kv = kv_ref[pl.ds(k0, tk), :]
        s = _nt_dot(qs_ref[...], kv[:, :FOX_HD]) * FOX_SCALE
        ck = _head_rows(ct_ref[0, :, pl.ds(k0, tk)], tq)
        cq = cq_ref[...]
        chunks = [(sc + cq) - kc for sc, kc in zip(_lane_chunks(s), _lane_chunks(ck))]
        if masked:
            chunks = _lane_chunks(_causal_mask(jnp.concatenate(chunks, axis=1), qi * tq, j * tk, tq))
        _flash_update(chunks, kv[:, FOX_HD:], m_ref, l_ref, acc_ref)

    n_full = (qi * tq + 1) // tk

    def body(j, carry):
        step(j, False)
        return carry

    lax.fori_loop(0, n_full, body, 0)
    step(n_full, True)
    _finish_fox(acc_ref, l_ref, g_ref, o_ref, tq)


def _mla_prompt(q, cos, sin, wuk, kmla, wuv, h, layer, batch, seq):
    nq = seq // Q_BLOCK
    rows = MLA_HEADS * Q_BLOCK
    tok = lambda blk: (lambda b, i: (b * nq + i, blk))
    par = lambda b, i: (layer, 0, 0, 0)
    return pl.pallas_call(
        _mla_prompt_kernel,
        out_shape=jax.ShapeDtypeStruct((batch * seq, MLA_W), BF16),
        grid=(batch, nq),
        in_specs=[pl.BlockSpec((Q_BLOCK, q.shape[1]), tok(0)),
                  pl.BlockSpec((Q_BLOCK, QK_ROPE), tok(0)),
                  pl.BlockSpec((Q_BLOCK, QK_ROPE), tok(0)),
                  pl.BlockSpec((None,) + wuk.shape[1:], par),
                  pl.BlockSpec((seq, MLA_QK), lambda b, i: (b, 0)),
                  pl.BlockSpec((None,) + wuv.shape[1:], par),
                  pl.BlockSpec((Q_BLOCK, MLA_W), tok(C_G_MLA // MLA_W))],
        out_specs=pl.BlockSpec((Q_BLOCK, MLA_W), tok(0)),
        scratch_shapes=[pltpu.VMEM((rows, MLA_QK), BF16), pltpu.VMEM((rows, LANES), F32),
                        pltpu.VMEM((rows, LANES), F32), pltpu.VMEM((rows, KV_LORA), F32)],
        compiler_params=_params("parallel", "arbitrary"),
        name="mla_prompt",
    )(q, cos, sin, wuk, kmla, wuv, h)


def _fox_prompt(h, fkv, c, ct, batch, seq):
    nq = seq // Q_BLOCK
    rows = FOX_HEADS * Q_BLOCK
    tok = lambda blk: (lambda b, i: (b * nq + i, blk))
    return pl.pallas_call(
        _fox_prompt_kernel,
        out_shape=jax.ShapeDtypeStruct((batch * seq, FOX_W), BF16),
        grid=(batch, nq),
        in_specs=[pl.BlockSpec((Q_BLOCK, FOX_W), tok(C_FQ // FOX_W)),
                  pl.BlockSpec((seq, 2 * FOX_HD), lambda b, i: (b, 0)),
                  pl.BlockSpec((Q_BLOCK, LANES), tok(0)),
                  pl.BlockSpec((1, HEAD_ROWS, seq), lambda b, i: (b, 0, 0)),
                  pl.BlockSpec((Q_BLOCK, FOX_W), tok(C_G_FOX // FOX_W))],
        out_specs=pl.BlockSpec((Q_BLOCK, FOX_W), tok(0)),
        scratch_shapes=[pltpu.VMEM((rows, FOX_HD), BF16), pltpu.VMEM((rows, LANES), F32),
                        pltpu.VMEM((rows, LANES), F32), pltpu.VMEM((rows, LANES), F32),
                        pltpu.VMEM((rows, FOX_HD), F32)],
        compiler_params=_params("parallel", "arbitrary"),
        name="fox_prompt",
    )(h, fkv, c, ct, h)


def _mem_attn_kernel(q_ref, k_ref, v_ref, g_ref, o_ref):
    for h in range(MEM_HEADS):
        cols = slice(h * MEM_HD, (h + 1) * MEM_HD)
        s = _nt_dot(q_ref[:, cols].astype(BF16), k_ref[:, cols].astype(BF16)) * MEM_SCALE
        p = jnp.exp(s - jnp.max(s, axis=-1, keepdims=True))
        inv = 1.0 / jnp.sum(p, axis=-1, keepdims=True)
        o = jnp.dot(p.astype(BF16), v_ref[:, cols].astype(BF16), preferred_element_type=F32)
        o_ref[:, cols] = (o * inv * _silu(g_ref[:, cols])).astype(o_ref.dtype)


def _mem_cache_spec(layer, slots):
    return pl.BlockSpec((None, slots, MEM_W), lambda b, i: (layer, b, 0))


def _mem_attn(h, k, v, k_spec, v_spec, batch, tq, steps, row0, out_dtype, name):
    r0 = row0 // tq
    tok = lambda blk: (lambda b, i: (r0 + b * steps + i, blk))
    return pl.pallas_call(
        _mem_attn_kernel,
        out_shape=jax.ShapeDtypeStruct((batch * steps * tq, MEM_W), out_dtype),
        grid=(batch, steps),
        in_specs=[pl.BlockSpec((tq, MEM_W), tok(C_MQ // MEM_W)), k_spec, v_spec,
                  pl.BlockSpec((tq, MEM_W), tok(C_G_MEM // MEM_W))],
        out_specs=pl.BlockSpec((tq, MEM_W), lambda b, i: (b * steps + i, 0)),
        compiler_params=_params("parallel", "arbitrary"),
        name=name,
    )(h, k, v, h)


def _fox_cum_sample_kernel(pt_ref, pool_ref, lfn_ref, ct_ref, cq_ref, buf_ref, sem_ref, *, layer, n_pages, dec_seq):
    b = pl.program_id(0)
    nb = pl.num_programs(0)
    slot = b % 2
    ppb = CUM_BLOCK // PAGE_SIZE
    n_blk = n_pages // ppb
    past = n_pages * PAGE_SIZE

    def page_copy(bb, i, sl):
        return pltpu.make_async_copy(
            pool_ref.at[layer, pt_ref[bb, i]],
            buf_ref.at[sl, i // ppb, :, pl.ds((i % ppb) * PAGE_SIZE, PAGE_SIZE)],
            sem_ref.at[sl])

    def issue(bb, sl):
        for i in range(n_pages):
            page_copy(bb, i, sl).start()

    @pl.when(b == 0)
    def _():
        issue(0, 0)

    @pl.when(b + 1 < nb)
    def _():
        issue(b + 1, 1 - slot)

    for i in range(n_pages):
        page_copy(b, i, slot).wait()

    rows = n_blk * HEAD_ROWS
    x = buf_ref[slot].reshape(rows, CUM_BLOCK)
    p = _prefix_lanes(x, _tri(CUM_BLOCK, upper=True))
    tot = jnp.broadcast_to(p[:, CUM_BLOCK - 1:], (rows, LANES))
    r = lax.broadcasted_iota(jnp.int32, (rows, rows), 0)
    c = lax.broadcasted_iota(jnp.int32, (rows, rows), 1)
    earlier = jnp.where(((r & (HEAD_ROWS - 1)) == (c & (HEAD_ROWS - 1))) & (c < r - (r & (HEAD_ROWS - 1))),
                        1.0, 0.0).astype(BF16)
    off = sum(jnp.dot(earlier, t, preferred_element_type=F32) for t in _split3(tot))
    cum = p + off[:, :1]
    for blk in range(n_blk):
        ct_ref[0, :, blk * CUM_BLOCK:(blk + 1) * CUM_BLOCK] = cum[blk * HEAD_ROWS:(blk + 1) * HEAD_ROWS, :]
    p_tot = cum[rows - HEAD_ROWS:, CUM_BLOCK - 1:]

    pad_rows = lambda a: jnp.concatenate([a, jnp.zeros((LANES - a.shape[0], LANES), F32)], axis=0)
    lfn_t = pad_rows(lfn_ref[...]).T[:HEAD_ROWS, :]
    c_new_t = _prefix_lanes(lfn_t, _tri(LANES, upper=True)) + p_tot
    ct_ref[0, :, past:past + LANES] = c_new_t
    c_new = pad_rows(c_new_t).T
    cq_ref[0] = jnp.concatenate(
        [jnp.broadcast_to(c_new[:dec_seq, h:h + 1], (dec_seq, LANES)) for h in range(FOX_HEADS)], axis=0)


def _fox_cum_sample(page_table, pool_t, logf, layer, row0, dec_seq):
    dec_batch, n_pages = page_table.shape
    past = n_pages * PAGE_SIZE
    n_blk = past // CUM_BLOCK
    kern = functools.partial(_fox_cum_sample_kernel, layer=layer, n_pages=n_pages, dec_seq=dec_seq)
    return pl.pallas_call(
        kern,
        out_shape=(jax.ShapeDtypeStruct((dec_batch, HEAD_ROWS, past + LANES), F32),
                   jax.ShapeDtypeStruct((dec_batch, FOX_HEADS * dec_seq, LANES), F32)),
        grid_spec=pltpu.PrefetchScalarGridSpec(
            num_scalar_prefetch=1,
            grid=(dec_batch,),
            in_specs=[pl.BlockSpec(memory_space=pl.ANY),
                      pl.BlockSpec((dec_seq, LANES), lambda b, pt: (row0 // dec_seq + b, 0))],
            out_specs=(pl.BlockSpec((1, HEAD_ROWS, past + LANES), lambda b, pt: (b, 0, 0)),
                       pl.BlockSpec((1, FOX_HEADS * dec_seq, LANES), lambda b, pt: (b, 0, 0))),
            scratch_shapes=[pltpu.VMEM((2, n_blk, HEAD_ROWS, CUM_BLOCK), F32), pltpu.SemaphoreType.DMA((2,))]),
        compiler_params=_params("arbitrary"),
        name="fox_cum_sample",
    )(page_table, pool_t, logf)


PAGES_PER_CHUNK = 32


def _sample_attn_kernel(pt_ref, q_ref, cos_ref, sin_ref, wuk_ref, wuv_ref, fq_ref,
                        ckvn_ref, krn_ref, fkn_ref, fvn_ref, ct_ref, ctn_ref, cq_ref, gm_ref, gf_ref,
                        ckv_pool, krt_pool, fk_pool, fv_pool,
                        om_ref, of_ref,
                        ckv_buf, krt_buf, fk_buf, fv_buf, sem_ref,
                        qm_ref, qf_ref, m1_ref, l1_ref, a1_ref, m2_ref, l2_ref, a2_ref,
                        *, layer, n_chunks, ppc, dec_seq):
    b = pl.program_id(0)
    j = pl.program_id(1)
    g = b * n_chunks + j
    total = pl.num_programs(0) * n_chunks
    slot = g % 2

    def page_copies(bb, jj, i, sl):
        pid = pt_ref[bb, jj * ppc + i]
        rows = pl.ds(i * PAGE_SIZE, PAGE_SIZE)
        return (pltpu.make_async_copy(ckv_pool.at[layer, pid], ckv_buf.at[sl, rows, :], sem_ref.at[0, sl]),
                pltpu.make_async_copy(krt_pool.at[layer, pid], krt_buf.at[sl, :, rows], sem_ref.at[1, sl]),
                pltpu.make_async_copy(fk_pool.at[layer, pid], fk_buf.at[sl, rows, :], sem_ref.at[2, sl]),
                pltpu.make_async_copy(fv_pool.at[layer, pid], fv_buf.at[sl, rows, :], sem_ref.at[3, sl]))

    def issue(bb, jj, sl):
        for i in range(ppc):
            for cp in page_copies(bb, jj, i, sl):
                cp.start()

    @pl.when(g == 0)
    def _():
        issue(0, 0, 0)

    @pl.when(g + 1 < total)
    def _():
        nxt = g + 1
        issue(nxt // n_chunks, nxt % n_chunks, 1 - slot)

    @pl.when(j == 0)
    def _():
        _build_mla_queries(q_ref, cos_ref[...], sin_ref[...], wuk_ref, qm_ref, dec_seq)
        for h in range(FOX_HEADS):
            qf_ref[h * dec_seq:(h + 1) * dec_seq, :] = fq_ref[:, h * FOX_HD:(h + 1) * FOX_HD].astype(BF16)
        _flash_init(m1_ref, l1_ref, a1_ref)
        _flash_init(m2_ref, l2_ref, a2_ref)

    for i in range(ppc):
        for cp in page_copies(b, j, i, slot):
            cp.wait()

    def fox_logits(fk, ck):
        s = _nt_dot(qf_ref[...], fk) * FOX_SCALE
        cq = cq_ref[0]
        return [(sc + cq) - kc for sc, kc in zip(_lane_chunks(s), _lane_chunks(_head_rows(ck, dec_seq)))]

    ckv = ckv_buf[slot].astype(BF16)
    qm = qm_ref[...]
    s1 = (_nt_dot(qm[:, :KV_LORA], ckv)
          + jnp.dot(qm[:, KV_LORA:], krt_buf[slot].astype(BF16), preferred_element_type=F32)) * MLA_SCALE
    _flash_update(_lane_chunks(s1), ckv, m1_ref, l1_ref, a1_ref)
    _flash_update(fox_logits(fk_buf[slot].astype(BF16), ct_ref[0]), fv_buf[slot].astype(BF16),
                  m2_ref, l2_ref, a2_ref)

    @pl.when(j == n_chunks - 1)
    def _():
        pad = lambda x: jnp.concatenate(
            [x, jnp.zeros((LANES - dec_seq, x.shape[1]), x.dtype)], axis=0).astype(BF16)
        rows = FOX_HEADS * dec_seq
        r = lax.broadcasted_iota(jnp.int32, (rows, LANES), 0)
        c = lax.broadcasted_iota(jnp.int32, (rows, LANES), 1)
        visible = c <= (r & (dec_seq - 1))
        ckv_n = pad(ckvn_ref[...])
        qmm = qm_ref[...]
        s1n = (_nt_dot(qmm[:, :KV_LORA], ckv_n) + _nt_dot(qmm[:, KV_LORA:], pad(krn_ref[...]))) * MLA_SCALE
        _flash_update([jnp.where(visible, s1n, NEG_BIG)], ckv_n, m1_ref, l1_ref, a1_ref)
        s2n, = fox_logits(pad(fkn_ref[...]), ctn_ref[0])
        _flash_update([jnp.where(visible, s2n, NEG_BIG)], pad(fvn_ref[...]), m2_ref, l2_ref, a2_ref)
        _finish_mla(a1_ref, l1_ref, wuv_ref, gm_ref, om_ref, dec_seq)
        _finish_fox(a2_ref, l2_ref, gf_ref, of_ref, dec_seq)


def _sample_attn(page_table, q, cos, sin, wuk, wuv, h, ckv, krope, ct, cq,
                 ckv_pool, krt_pool, fk_pool, fv_pool, layer, row0, dec_seq):
    dec_batch, n_pages = page_table.shape
    ppc = PAGES_PER_CHUNK if n_pages % PAGES_PER_CHUNK == 0 else n_pages
    n_chunks = n_pages // ppc
    tkc = ppc * PAGE_SIZE
    rows = MLA_HEADS * dec_seq
    r0 = row0 // dec_seq
    tok = lambda blk: (lambda b, j, pt: (r0 + b, blk))
    par = lambda b, j, pt: (layer, 0, 0, 0)
    anyspec = pl.BlockSpec(memory_space=pl.ANY)
    kern = functools.partial(_sample_attn_kernel, layer=layer, n_chunks=n_chunks, ppc=ppc, dec_seq=dec_seq)
    return pl.pallas_call(
        kern,
        out_shape=(jax.ShapeDtypeStruct((dec_batch * dec_seq, MLA_W), F32),
                   jax.ShapeDtypeStruct((dec_batch * dec_seq, FOX_W), F32)),
        grid_spec=pltpu.PrefetchScalarGridSpec(
            num_scalar_prefetch=1,
            grid=(dec_batch, n_chunks),
            in_specs=[pl.BlockSpec((dec_seq, q.shape[1]), tok(0)),
                      pl.BlockSpec((dec_seq, QK_ROPE), tok(0)),
                      pl.BlockSpec((dec_seq, QK_ROPE), tok(0)),
                      pl.BlockSpec((None,) + wuk.shape[1:], par),
                      pl.BlockSpec((None,) + wuv.shape[1:], par),
                      pl.BlockSpec((dec_seq, FOX_W), tok(C_FQ // FOX_W)),
                      pl.BlockSpec((dec_seq, KV_LORA), tok(0)),
                      pl.BlockSpec((dec_seq, QK_ROPE), tok(0)),
                      pl.BlockSpec((dec_seq, FOX_HD), tok(C_FK // FOX_HD)),
                      pl.BlockSpec((dec_seq, FOX_HD), tok(C_FV // FOX_HD)),
                      pl.BlockSpec((1, HEAD_ROWS, tkc), lambda b, j, pt: (b, 0, j)),
                      pl.BlockSpec((1, HEAD_ROWS, LANES), lambda b, j, pt: (b, 0, n_pages)),
                      pl.BlockSpec((1, rows, LANES), lambda b, j, pt: (b, 0, 0)),
                      pl.BlockSpec((dec_seq, MLA_W), tok(C_G_MLA // MLA_W)),
                      pl.BlockSpec((dec_seq, FOX_W), tok(C_G_FOX // FOX_W)),
                      anyspec, anyspec, anyspec, anyspec],
            out_specs=(pl.BlockSpec((dec_seq, MLA_W), lambda b, j, pt: (b, 0)),
                       pl.BlockSpec((dec_seq, FOX_W), lambda b, j, pt: (b, 0))),
            scratch_shapes=[pltpu.VMEM((2, tkc, KV_LORA), F32), pltpu.VMEM((2, QK_ROPE, tkc), F32),
                            pltpu.VMEM((2, tkc, FOX_HD), F32), pltpu.VMEM((2, tkc, FOX_HD), F32),
                            pltpu.SemaphoreType.DMA((4, 2)),
                            pltpu.VMEM((rows, MLA_QK), BF16), pltpu.VMEM((rows, FOX_HD), BF16),
                            pltpu.VMEM((rows, LANES), F32), pltpu.VMEM((rows, LANES), F32),
                            pltpu.VMEM((rows, KV_LORA), F32),
                            pltpu.VMEM((rows, LANES), F32), pltpu.VMEM((rows, LANES), F32),
                            pltpu.VMEM((rows, FOX_HD), F32)]),
        compiler_params=_params("arbitrary", "arbitrary"),
        name="sample_attn",
    )(page_table, q, cos, sin, wuk, wuv, h, ckv, krope, h, h, ct, ct, cq, h, h,
      ckv_pool, krt_pool, fk_pool, fv_pool)


def _ln_kernel(x_ref, y_ref, g_ref, b_ref, o_ref, ob_ref, *, alpha):
    v = alpha * x_ref[...] + y_ref[...]
    mu = jnp.mean(v, axis=-1, keepdims=True)
    vc = v - mu
    var = jnp.mean(vc * vc, axis=-1, keepdims=True)
    out = vc * lax.rsqrt(var + LN_EPS) * g_ref[...] + b_ref[...]
    o_ref[...] = out
    ob_ref[...] = out.astype(BF16)


def _layernorm(x, y, g, b, layer, alpha):
    m, d = x.shape
    tm = _pick(m, 256)
    row = lambda i: (i, 0)
    par = lambda i: (layer, 0, 0)
    return pl.pallas_call(
        functools.partial(_ln_kernel, alpha=alpha),
        out_shape=(jax.ShapeDtypeStruct((m, d), F32), jax.ShapeDtypeStruct((m, d), BF16)),
        grid=(m // tm,),
        in_specs=[pl.BlockSpec((tm, d), row), pl.BlockSpec((tm, d), row),
                  pl.BlockSpec((None, 1, d), par), pl.BlockSpec((None, 1, d), par)],
        out_specs=(pl.BlockSpec((tm, d), row), pl.BlockSpec((tm, d), row)),
        compiler_params=_params("parallel"),
        name="residual_layernorm",
    )(x, y, g, b)


def _split_cols(w):
    offs = [0]
    for s in IN_SIZES:
        offs.append(offs[-1] + s)
    return [w[..., offs[i]:offs[i + 1]] for i in range(len(IN_SIZES))]


def _arrange_w_in(w_in):
    q_a, kv_a, k_pe, fq, fk, fv, f_logit, mq, gate = _split_cols(w_in)
    lead = w_in.shape[:-1]
    zeros = lambda n: jnp.zeros(lead + (n,), w_in.dtype)
    misc = jnp.concatenate([f_logit, zeros(LANES - FOX_HEADS - QK_ROPE), k_pe], axis=-1)
    parts = [fq, gate[..., :MLA_W], gate[..., MLA_W:MLA_W + FOX_W], q_a, kv_a, fk, fv, misc, zeros(LANES),
             gate[..., MLA_W + FOX_W:], mq]
    out = jnp.concatenate(parts, axis=-1).astype(BF16)
    assert out.shape[-1] == N_PROJ
    return out


def _arrange_w_q_up(w_q_up):
    depth, lora, _ = w_q_up.shape
    w = w_q_up.reshape(depth, lora, MLA_HEADS, QK_NOPE + QK_ROPE)
    nope = w[..., :QK_NOPE].reshape(depth, lora, MLA_HEADS * QK_NOPE)
    rope = w[..., QK_NOPE:].reshape(depth, lora, MLA_HEADS * QK_ROPE)
    return jnp.concatenate([nope, rope], axis=-1).astype(BF16)


def _rope_tables(positions):
    half = QK_ROPE // 2
    inv = ROPE_THETA ** (-jnp.arange(half, dtype=F32) / half)
    ang = positions.astype(F32)[:, None] * inv[None, :]
    cos, sin = jnp.cos(ang), jnp.sin(ang)
    return jnp.concatenate([cos, cos], axis=-1), jnp.concatenate([sin, sin], axis=-1)


def kernel(x_prompt, x_sample, cache_mla_ckv, cache_mla_krope, cache_fox_k, cache_fox_v, cache_fox_logf,
           cache_mem_k, cache_mem_v, page_table, mem_prompt, w_in, b_f, q_a_norm, w_q_up, kv_a_norm, w_kv_up,
           w_mem_k, w_mem_v, w_out, ln_g, ln_b):
    batch, seq, d_model = x_prompt.shape
    dec_batch, dec_seq, _ = x_sample.shape
    depth = w_in.shape[0]
    n_pages = page_table.shape[1]
    past = n_pages * PAGE_SIZE
    mem_slots = mem_prompt.shape[1]
    assert w_in.shape[-1] == sum(IN_SIZES) and cache_mla_ckv.shape[2] == PAGE_SIZE
    assert seq % PROMPT_TK == 0 and PROMPT_TK % Q_BLOCK == 0 and dec_seq % SUBLANES == 0
    assert past % CUM_BLOCK == 0 and CUM_BLOCK % PAGE_SIZE == 0
    alpha = (2 * depth) ** 0.25
    n_p = batch * seq
    n_s = dec_batch * dec_seq

    w_in_r = _arrange_w_in(w_in)
    w_q_r = _arrange_w_q_up(w_q_up)
    wuk = jnp.transpose(w_kv_up[..., :QK_NOPE], (0, 2, 3, 1)).astype(BF16)
    wuv = jnp.transpose(w_kv_up[..., QK_NOPE:], (0, 2, 1, 3)).astype(BF16)
    w_mem = jnp.concatenate([w_mem_k, w_mem_v], axis=-1).astype(BF16)
    w_out_b = w_out.astype(BF16)
    b_f_pad = jnp.pad(b_f, ((0, 0), (0, LANES - FOX_HEADS)))[:, None, :]
    q_norm, kv_norm = q_a_norm[:, None, :], kv_a_norm[:, None, :]
    ln_g3, ln_b3 = ln_g[:, None, :], ln_b[:, None, :]
    mem_b = mem_prompt.reshape(batch * mem_slots, d_model).astype(BF16)
    krt_pool = jnp.swapaxes(cache_mla_krope, 2, 3)
    lft_pool = jnp.pad(jnp.swapaxes(cache_fox_logf, 2, 3),
                       ((0, 0), (0, 0), (0, HEAD_ROWS - FOX_HEADS), (0, 0)))
    mem_k_s = cache_mem_k.reshape(depth, dec_batch * mem_slots, MEM_W)
    mem_v_s = cache_mem_v.reshape(depth, dec_batch * mem_slots, MEM_W)

    pos = jnp.concatenate([jnp.tile(jnp.arange(seq), batch), jnp.tile(past + jnp.arange(dec_seq), dec_batch)])
    cos, sin = _rope_tables(pos)

    x = jnp.concatenate([x_prompt.reshape(n_p, d_model), x_sample.reshape(n_s, d_model)], axis=0)
    xb = x.astype(BF16)

    tq_mem = _pick(seq, 512)
    outs = {k: [] for k in ("p_ckv", "p_kr", "p_fk", "p_fv", "p_lf", "p_mk", "p_mv",
                            "s_ckv", "s_kr", "s_fk", "s_fv", "s_lf")}
    for l in range(depth):
        h = _linear(xb, w_in_r, l, F32, "proj_in")
        qan, ckv, krope, kmla, fkv, logf = _prep(h, q_norm, kv_norm, b_f_pad, cos, sin, l)
        q = _linear(qan, w_q_r, l, F32, "q_up")

        c, ct = _cumsum_prompt(logf, batch, seq)
        o_mla_p = _mla_prompt(q, cos, sin, wuk, kmla, wuv, h, l, batch, seq)
        o_fox_p = _fox_prompt(h, fkv, c, ct, batch, seq)
        mkv = _linear(mem_b, w_mem, l, F32, "mem_kv")
        o_mem_p = _mem_attn(h, mkv, mkv,
                            pl.BlockSpec((mem_slots, MEM_W), lambda b, i: (b, 0)),
                            pl.BlockSpec((mem_slots, MEM_W), lambda b, i: (b, 1)),
                            batch, tq_mem, seq // tq_mem, 0, BF16, "mem_attn_prompt")

        ct_s, cq_s = _fox_cum_sample(page_table, lft_pool, logf, l, n_p, dec_seq)
        o_mla_s, o_fox_s = _sample_attn(page_table, q, cos, sin, wuk, wuv, h, ckv, krope, ct_s, cq_s,
                                        cache_mla_ckv, krt_pool, cache_fox_k, cache_fox_v, l, n_p, dec_seq)
        mem_spec = _mem_cache_spec(l, mem_slots)
        o_mem_s = _mem_attn(h, mem_k_s, mem_v_s, mem_spec, mem_spec, dec_batch, dec_seq, 1, n_p, F32,
                            "mem_attn_sample")
        a_s = jnp.concatenate([o_mla_s, o_fox_s, o_mem_s], axis=1).astype(BF16)

        y = _proj_out(o_mla_p, o_fox_p, o_mem_p, a_s, w_out_b, l)
        x, xb = _layernorm(x, y, ln_g3, ln_b3, l, alpha)

        fk, fv = h[:, C_FK:C_FK + FOX_HD], h[:, C_FV:C_FV + FOX_HD]
        lf = logf[:, :FOX_HEADS]
        outs["p_ckv"].append(ckv[:n_p].reshape(batch, seq, KV_LORA))
        outs["p_kr"].append(krope[:n_p].reshape(batch, seq, QK_ROPE))
        outs["p_fk"].append(fk[:n_p].reshape(batch, seq, FOX_HD))
        outs["p_fv"].append(fv[:n_p].reshape(batch, seq, FOX_HD))
        outs["p_lf"].append(lf[:n_p].reshape(batch, seq, FOX_HEADS))
        outs["p_mk"].append(mkv[:, :MEM_W].reshape(batch, mem_slots, MEM_HEADS, MEM_HD))
        outs["p_mv"].append(mkv[:, MEM_W:].reshape(batch, mem_slots, MEM_HEADS, MEM_HD))
        outs["s_ckv"].append(ckv[n_p:].reshape(dec_batch, dec_seq, KV_LORA))
        outs["s_kr"].append(krope[n_p:].reshape(dec_batch, dec_seq, QK_ROPE))
        outs["s_fk"].append(fk[n_p:].reshape(dec_batch, dec_seq, FOX_HD))
        outs["s_fv"].append(fv[n_p:].reshape(dec_batch, dec_seq, FOX_HD))
        outs["s_lf"].append(lf[n_p:].reshape(dec_batch, dec_seq, FOX_HEADS))

    st = {k: jnp.stack(v) for k, v in outs.items()}
    return (x[:n_p].reshape(batch, seq, d_model), x[n_p:].reshape(dec_batch, dec_seq, d_model),
            st["p_ckv"], st["p_kr"], st["p_fk"], st["p_fv"], st["p_lf"], st["p_mk"], st["p_mv"],
            st["s_ckv"], st["s_kr"], st["s_fk"], st["s_fv"], st["s_lf"])
```

```python
import functools
import math

import jax
import jax.numpy as jnp
from jax import lax
from jax.experimental import pallas as pl
from jax.experimental.pallas import tpu as pltpu

F32 = jnp.float32
BF16 = jnp.bfloat16

MLA_HEADS = 12
Q_LORA = 768
KV_LORA = 256
QK_NOPE = 128
QK_ROPE = 64
MLA_V = 128
ROPE_THETA = 10000.0
MLA_SCALE = (QK_NOPE + QK_ROPE) ** -0.5
FOX_HEADS = 12
FOX_HD = 128
FOX_SCALE = FOX_HD ** -0.5
MEM_HEADS = 4
MEM_HD = 256
MEM_SCALE = MEM_HD ** -0.5
MLA_W = MLA_HEADS * MLA_V
FOX_W = FOX_HEADS * FOX_HD
MEM_W = MEM_HEADS * MEM_HD
MIX_W = MLA_W + FOX_W + MEM_W
IN_SIZES = (Q_LORA, KV_LORA, QK_ROPE, FOX_W, FOX_HD, FOX_HD, FOX_HEADS, MEM_W, MIX_W)
RMS_EPS = 1e-6
LN_EPS = 1e-5
PAGE_SIZE = 128
Q_BLOCK = 128
MLA_QK = KV_LORA + QK_ROPE

LANES = 128
SUBLANES = 8
HEAD_ROWS = 2 * SUBLANES
VMEM_LIMIT = 48 * 1024 * 1024
WEIGHT_TILE_BYTES = 4 * 1024 * 1024
NEG_BIG = -1e30

C_FQ = 0
C_G_MLA = C_FQ + FOX_W
C_G_FOX = C_G_MLA + MLA_W
C_QA = C_G_FOX + FOX_W
C_KVA = C_QA + Q_LORA
C_FK = C_KVA + KV_LORA
C_FV = C_FK + FOX_HD
C_MISC = C_FV + FOX_HD
C_PAD = C_MISC + LANES
C_G_MEM = C_PAD + LANES
C_MQ = C_G_MEM + MEM_W
N_PROJ = C_MQ + MEM_W
MISC_ROPE = LANES - QK_ROPE


def _pick(dim, pref):
    if dim <= pref:
        return dim
    t = (pref // LANES) * LANES
    while t > LANES and dim % t:
        t -= LANES
    return t if dim % t == 0 else dim


def _params(*semantics):
    return pltpu.CompilerParams(dimension_semantics=semantics, vmem_limit_bytes=VMEM_LIMIT)


def _linear_kernel(x_ref, w_ref, o_ref):
    o_ref[...] = jnp.dot(x_ref[...], w_ref[...], preferred_element_type=F32).astype(o_ref.dtype)


def _linear(x, w, layer, out_dtype, name):
    m, k = x.shape
    n = w.shape[-1]
    tm = _pick(m, 1024)
    tn = _pick(n, max(LANES, min(1024, WEIGHT_TILE_BYTES // (2 * k))))
    return pl.pallas_call(
        _linear_kernel,
        out_shape=jax.ShapeDtypeStruct((m, n), out_dtype),
        grid=(m // tm, n // tn),
        in_specs=[pl.BlockSpec((tm, k), lambda i, j: (i, 0)),
                  pl.BlockSpec((None, k, tn), lambda i, j: (layer, 0, j))],
        out_specs=pl.BlockSpec((tm, tn), lambda i, j: (i, j)),
        compiler_params=_params("parallel", "arbitrary"),
        name=name,
    )(x, w)


def _proj_out_kernel(p1_ref, p2_ref, p3_ref, s1_ref, s2_ref, s3_ref, w_ref, o_ref, *, n_prompt_tiles):
    def run(r1, r2, r3):
        dot = lambda a, lo, hi: jnp.dot(a[...], w_ref[lo:hi, :], preferred_element_type=F32)
        o_ref[...] = dot(r1, 0, MLA_W) + dot(r2, MLA_W, MLA_W + FOX_W) + dot(r3, MLA_W + FOX_W, MIX_W)

    i = pl.program_id(0)

    @pl.when(i < n_prompt_tiles)
    def _():
        run(p1_ref, p2_ref, p3_ref)

    @pl.when(i >= n_prompt_tiles)
    def _():
        run(s1_ref, s2_ref, s3_ref)


def _proj_out(o_mla_p, o_fox_p, o_mem_p, a_s, w_out, layer):
    n_p, n_s = o_mla_p.shape[0], a_s.shape[0]
    d = w_out.shape[-1]
    tm = _pick(math.gcd(n_p, n_s), 512)
    tn = _pick(d, 512)
    npt = n_p // tm
    prow = lambda i, j: (jnp.minimum(i, npt - 1), 0)
    srow = lambda blk: (lambda i, j: (jnp.maximum(i - npt, 0), blk))
    return pl.pallas_call(
        functools.partial(_proj_out_kernel, n_prompt_tiles=npt),
        out_shape=jax.ShapeDtypeStruct((n_p + n_s, d), F32),
        grid=((n_p + n_s) // tm, d // tn),
        in_specs=[pl.BlockSpec((tm, MLA_W), prow), pl.BlockSpec((tm, FOX_W), prow), pl.BlockSpec((tm, MEM_W), prow),
                  pl.BlockSpec((tm, MLA_W), srow(0)), pl.BlockSpec((tm, FOX_W), srow(MLA_W // FOX_W)),
                  pl.BlockSpec((tm, MEM_W), srow((MLA_W + FOX_W) // MEM_W)),
                  pl.BlockSpec((None, MIX_W, tn), lambda i, j: (layer, 0, j))],
        out_specs=pl.BlockSpec((tm, tn), lambda i, j: (i, j)),
        compiler_params=_params("parallel", "arbitrary"),
        name="proj_out",
    )(o_mla_p, o_fox_p, o_mem_p, a_s, a_s, a_s, w_out)


def _rope(x, cos, sin):
    half = QK_ROPE // 2
    rot = jnp.concatenate([-x[:, half:], x[:, :half]], axis=-1)
    return x * cos + rot * sin


def _rms(x, g):
    return x * lax.rsqrt(jnp.mean(x * x, axis=-1, keepdims=True) + RMS_EPS) * g


def _prep_kernel(qa_ref, kva_ref, fk_ref, fv_ref, misc_ref, qg_ref, kg_ref, bf_ref, cos_ref, sin_ref,
                 qan_ref, ckv_ref, krope_ref, kmla_ref, fkv_ref, logf_ref):
    qan_ref[...] = _rms(qa_ref[...], qg_ref[...]).astype(BF16)
    ckv = _rms(kva_ref[...], kg_ref[...])
    ckv_ref[...] = ckv
    misc = misc_ref[...]
    kr = _rope(misc[:, MISC_ROPE:], cos_ref[...], sin_ref[...])
    krope_ref[...] = kr
    kmla_ref[:, :KV_LORA] = ckv.astype(BF16)
    kmla_ref[:, KV_LORA:] = kr.astype(BF16)
    fkv_ref[:, :FOX_HD] = fk_ref[...].astype(BF16)
    fkv_ref[:, FOX_HD:] = fv_ref[...].astype(BF16)
    z = misc + bf_ref[...]
    lf = -(jnp.maximum(-z, 0.0) + jnp.log1p(jnp.exp(-jnp.abs(z))))
    lane = lax.broadcasted_iota(jnp.int32, lf.shape, 1)
    logf_ref[...] = jnp.where(lane < FOX_HEADS, lf, 0.0)


def _prep(h, q_norm, kv_norm, b_f, cos, sin, layer):
    m = h.shape[0]
    tm = _pick(m, 512)
    row = lambda blk: (lambda i: (i, blk))
    par = lambda i: (layer, 0, 0)
    return pl.pallas_call(
        _prep_kernel,
        out_shape=(jax.ShapeDtypeStruct((m, Q_LORA), BF16),
                   jax.ShapeDtypeStruct((m, KV_LORA), F32),
                   jax.ShapeDtypeStruct((m, QK_ROPE), F32),
                   jax.ShapeDtypeStruct((m, MLA_QK), BF16),
                   jax.ShapeDtypeStruct((m, 2 * FOX_HD), BF16),
                   jax.ShapeDtypeStruct((m, LANES), F32)),
        grid=(m // tm,),
        in_specs=[pl.BlockSpec((tm, Q_LORA), row(C_QA // Q_LORA)),
                  pl.BlockSpec((tm, KV_LORA), row(C_KVA // KV_LORA)),
                  pl.BlockSpec((tm, FOX_HD), row(C_FK // FOX_HD)),
                  pl.BlockSpec((tm, FOX_HD), row(C_FV // FOX_HD)),
                  pl.BlockSpec((tm, LANES), row(C_MISC // LANES)),
                  pl.BlockSpec((None, 1, Q_LORA), par),
                  pl.BlockSpec((None, 1, KV_LORA), par),
                  pl.BlockSpec((None, 1, LANES), par),
                  pl.BlockSpec((tm, QK_ROPE), row(0)),
                  pl.BlockSpec((tm, QK_ROPE), row(0))],
        out_specs=(pl.BlockSpec((tm, Q_LORA), row(0)),
                   pl.BlockSpec((tm, KV_LORA), row(0)),
                   pl.BlockSpec((tm, QK_ROPE), row(0)),
                   pl.BlockSpec((tm, MLA_QK), row(0)),
                   pl.BlockSpec((tm, 2 * FOX_HD), row(0)),
                   pl.BlockSpec((tm, LANES), row(0))),
        compiler_params=_params("parallel"),
        name="prep",
    )(h, h, h, h, h, q_norm, kv_norm, b_f, cos, sin)


CUM_BLOCK = 256


def _tri(n, upper):
    r = lax.broadcasted_iota(jnp.int32, (n, n), 0)
    c = lax.broadcasted_iota(jnp.int32, (n, n), 1)
    return jnp.where((r <= c) if upper else (r >= c), 1.0, 0.0).astype(BF16)


def _split3(x):
    x1 = x.astype(BF16)
    r1 = x - x1.astype(F32)
    x2 = r1.astype(BF16)
    x3 = (r1 - x2.astype(F32)).astype(BF16)
    return x1, x2, x3


def _prefix_rows(tri_lower, x):
    return sum(jnp.dot(tri_lower, t, preferred_element_type=F32) for t in _split3(x))


def _prefix_lanes(x, tri_upper):
    return sum(jnp.dot(t, tri_upper, preferred_element_type=F32) for t in _split3(x))


def _cumsum_prompt_kernel(lf_ref, c_ref, ct_ref):
    t = lf_ref.shape[0]
    tri = _tri(CUM_BLOCK, upper=False)
    carry = jnp.zeros((1, LANES), F32)
    for i in range(t // CUM_BLOCK):
        rows = slice(i * CUM_BLOCK, (i + 1) * CUM_BLOCK)
        c = _prefix_rows(tri, lf_ref[rows, :]) + carry
        c_ref[rows, :] = c
        ct_ref[0, :, rows] = c.T[:HEAD_ROWS, :]
        carry = c[CUM_BLOCK - 1:, :]


def _cumsum_prompt(logf, batch, seq):
    return pl.pallas_call(
        _cumsum_prompt_kernel,
        out_shape=(jax.ShapeDtypeStruct((batch * seq, LANES), F32),
                   jax.ShapeDtypeStruct((batch, HEAD_ROWS, seq), F32)),
        grid=(batch,),
        in_specs=[pl.BlockSpec((seq, LANES), lambda b: (b, 0))],
        out_specs=(pl.BlockSpec((seq, LANES), lambda b: (b, 0)),
                   pl.BlockSpec((1, HEAD_ROWS, seq), lambda b: (b, 0, 0))),
        compiler_params=_params("parallel"),
        name="cumsum_prompt",
    )(logf)


def _lane_chunks(s):
    return [s[:, c * LANES:(c + 1) * LANES] for c in range(s.shape[1] // LANES)]


def _flash_update(chunks, v, m_ref, l_ref, acc_ref):
    m_prev = m_ref[...]
    cmax = functools.reduce(jnp.maximum, chunks)
    m_new = jnp.maximum(m_prev, jnp.max(cmax, axis=-1, keepdims=True))
    alpha = jnp.exp(m_prev - m_new)
    ps = [jnp.exp(c - m_new) for c in chunks]
    l_ref[...] = alpha * l_ref[...] + functools.reduce(jnp.add, ps)
    p = jnp.concatenate([x.astype(BF16) for x in ps], axis=1)
    pv = jnp.dot(p, v, preferred_element_type=F32)
    for c in range(acc_ref.shape[1] // LANES):
        cols = slice(c * LANES, (c + 1) * LANES)
        acc_ref[:, cols] = alpha * acc_ref[:, cols] + pv[:, cols]
    m_ref[...] = m_new


def _flash_init(m_ref, l_ref, acc_ref):
    m_ref[...] = jnp.full(m_ref.shape, NEG_BIG, F32)
    l_ref[...] = jnp.zeros_like(l_ref)
    acc_ref[...] = jnp.zeros_like(acc_ref)


def _row_scale(l_ref):
    return 1.0 / jnp.sum(l_ref[...], axis=-1, keepdims=True)


def _nt_dot(a, b):
    return lax.dot_general(a, b, (((1,), (1,)), ((), ())), preferred_element_type=F32)


def _silu(g):
    return g * (1.0 / (1.0 + jnp.exp(-g)))


def _build_mla_queries(q_ref, cos, sin, wuk_ref, qs_ref, rows):
    for h in range(MLA_HEADS):
        qn = q_ref[:, h * QK_NOPE:(h + 1) * QK_NOPE].astype(BF16)
        ql = jnp.dot(qn, wuk_ref[h], preferred_element_type=F32)
        qs_ref[h * rows:(h + 1) * rows, :KV_LORA] = ql.astype(BF16)
        off = MLA_HEADS * QK_NOPE + h * QK_ROPE
        qs_ref[h * rows:(h + 1) * rows, KV_LORA:] = _rope(q_ref[:, off:off + QK_ROPE], cos, sin).astype(BF16)


def _finish_mla(acc_ref, l_ref, wuv_ref, g_ref, o_ref, rows):
    inv = _row_scale(l_ref)
    for h in range(MLA_HEADS):
        sl = slice(h * rows, (h + 1) * rows)
        o_lat = (acc_ref[sl, :] * inv[sl, :]).astype(BF16)
        o = jnp.dot(o_lat, wuv_ref[h], preferred_element_type=F32)
        cols = slice(h * MLA_V, (h + 1) * MLA_V)
        o_ref[:, cols] = (o * _silu(g_ref[:, cols])).astype(o_ref.dtype)


def _finish_fox(acc_ref, l_ref, g_ref, o_ref, rows):
    inv = _row_scale(l_ref)
    for h in range(FOX_HEADS):
        sl = slice(h * rows, (h + 1) * rows)
        cols = slice(h * FOX_HD, (h + 1) * FOX_HD)
        o_ref[:, cols] = (acc_ref[sl, :] * inv[sl, :] * _silu(g_ref[:, cols])).astype(o_ref.dtype)


def _causal_mask(s, q0, k0, rows):
    r = lax.broadcasted_iota(jnp.int32, s.shape, 0)
    c = lax.broadcasted_iota(jnp.int32, s.shape, 1)
    return jnp.where(q0 + (r & (rows - 1)) >= k0 + c, s, NEG_BIG)


def _head_rows(x, rows):
    return jnp.concatenate([jnp.broadcast_to(x[h:h + 1, :], (rows, x.shape[1])) for h in range(FOX_HEADS)], axis=0)


PROMPT_TK = 256


def _causal_sweep(qi, tq, tk, step):
    n_full = (qi * tq + 1) // tk

    def body(jj, carry):
        step(2 * jj, False)
        step(2 * jj + 1, False)
        return carry

    lax.fori_loop(0, n_full // 2, body, 0)

    @pl.when(n_full % 2 == 1)
    def _():
        step(n_full - 1, False)

    step(n_full, True)


def _mla_prompt_kernel(q_ref, cos_ref, sin_ref, wuk_ref, k_ref, wuv_ref, g_ref, o_ref,
                       qs_ref, m_ref, l_ref, acc_ref):
    qi = pl.program_id(1)
    tq, tk = Q_BLOCK, PROMPT_TK
    _build_mla_queries(q_ref, cos_ref[...], sin_ref[...], wuk_ref, qs_ref, tq)
    _flash_init(m_ref, l_ref, acc_ref)

    def step(j, masked):
        k = k_ref[pl.ds(pl.multiple_of(j * tk, tk), tk), :]
        s = _nt_dot(qs_ref[...], k) * MLA_SCALE
        if masked:
            s = _causal_mask(s, qi * tq, j * tk, tq)
        _flash_update(_lane_chunks(s), k[:, :KV_LORA], m_ref, l_ref, acc_ref)

    _causal_sweep(qi, tq, tk, step)
    _finish_mla(acc_ref, l_ref, wuv_ref, g_ref, o_ref, tq)


def _fox_prompt_kernel(q_ref, kv_ref, c_ref, ct_ref, g_ref, o_ref, qs_ref, cq_ref, m_ref, l_ref, acc_ref):
    qi = pl.program_id(1)
    tq, tk = Q_BLOCK, PROMPT_TK
    for h in range(FOX_HEADS):
        rows = slice(h * tq, (h + 1) * tq)
        qs_ref[rows, :] = q_ref[:, h * FOX_HD:(h + 1) * FOX_HD].astype(BF16)
        cq_ref[rows, :] = jnp.broadcast_to(c_ref[:, h:h + 1], (tq, LANES))
    _flash_init(m_ref, l_ref, acc_ref)

    def step(j, masked):
        k0 = pl.multiple_of(j * tk, tk)
        kv = kv_ref[pl.ds(k0, tk), :]
        s = _nt_dot(qs_ref[...], kv[:, :FOX_HD]) * FOX_SCALE
        ck = _head_rows(ct_ref[0, :, pl.ds(k0, tk)], tq)
        cq = cq_ref[...]
        chunks = [(sc + cq) - kc for sc, kc in zip(_lane_chunks(s), _lane_chunks(ck))]
        if masked:
            chunks = _lane_chunks(_causal_mask(jnp.concatenate(chunks, axis=1), qi * tq, j * tk, tq))
        _flash_update(chunks, kv[:, FOX_HD:], m_ref, l_ref, acc_ref)

    _causal_sweep(qi, tq, tk, step)
    _finish_fox(acc_ref, l_ref, g_ref, o_ref, tq)


def _mla_prompt(q, cos, sin, wuk, kmla, wuv, h, layer, batch, seq):
    nq = seq // Q_BLOCK
    rows = MLA_HEADS * Q_BLOCK
    tok = lambda blk: (lambda b, i: (b * nq + i, blk))
    par = lambda b, i: (layer, 0, 0, 0)
    return pl.pallas_call(
        _mla_prompt_kernel,
        out_shape=jax.ShapeDtypeStruct((batch * seq, MLA_W), BF16),
        grid=(batch, nq),
        in_specs=[pl.BlockSpec((Q_BLOCK, q.shape[1]), tok(0)),
                  pl.BlockSpec((Q_BLOCK, QK_ROPE), tok(0)),
                  pl.BlockSpec((Q_BLOCK, QK_ROPE), tok(0)),
                  pl.BlockSpec((None,) + wuk.shape[1:], par),
                  pl.BlockSpec((seq, MLA_QK), lambda b, i: (b, 0)),
                  pl.BlockSpec((None,) + wuv.shape[1:], par),
                  pl.BlockSpec((Q_BLOCK, MLA_W), tok(C_G_MLA // MLA_W))],
        out_specs=pl.BlockSpec((Q_BLOCK, MLA_W), tok(0)),
        scratch_shapes=[pltpu.VMEM((rows, MLA_QK), BF16), pltpu.VMEM((rows, LANES), F32),
                        pltpu.VMEM((rows, LANES), F32), pltpu.VMEM((rows, KV_LORA), F32)],
        compiler_params=_params("parallel", "arbitrary"),
        name="mla_prompt",
    )(q, cos, sin, wuk, kmla, wuv, h)


def _fox_prompt(h, fkv, c, ct, batch, seq):
    nq = seq // Q_BLOCK
    rows = FOX_HEADS * Q_BLOCK
    tok = lambda blk: (lambda b, i: (b * nq + i, blk))
    return pl.pallas_call(
        _fox_prompt_kernel,
        out_shape=jax.ShapeDtypeStruct((batch * seq, FOX_W), BF16),
        grid=(batch, nq),
        in_specs=[pl.BlockSpec((Q_BLOCK, FOX_W), tok(C_FQ // FOX_W)),
                  pl.BlockSpec((seq, 2 * FOX_HD), lambda b, i: (b, 0)),
                  pl.BlockSpec((Q_BLOCK, LANES), tok(0)),
                  pl.BlockSpec((1, HEAD_ROWS, seq), lambda b, i: (b, 0, 0)),
                  pl.BlockSpec((Q_BLOCK, FOX_W), tok(C_G_FOX // FOX_W))],
        out_specs=pl.BlockSpec((Q_BLOCK, FOX_W), tok(0)),
        scratch_shapes=[pltpu.VMEM((rows, FOX_HD), BF16), pltpu.VMEM((rows, LANES), F32),
                        pltpu.VMEM((rows, LANES), F32), pltpu.VMEM((rows, LANES), F32),
                        pltpu.VMEM((rows, FOX_HD), F32)],
        compiler_params=_params("parallel", "arbitrary"),
        name="fox_prompt",
    )(h, fkv, c, ct, h)


def _mem_attn_kernel(q_ref, k_ref, v_ref, g_ref, o_ref):
    for h in range(MEM_HEADS):
        cols = slice(h * MEM_HD, (h + 1) * MEM_HD)
        s = _nt_dot(q_ref[:, cols].astype(BF16), k_ref[:, cols].astype(BF16)) * MEM_SCALE
        p = jnp.exp(s - jnp.max(s, axis=-1, keepdims=True))
        inv = 1.0 / jnp.sum(p, axis=-1, keepdims=True)
        o = jnp.dot(p.astype(BF16), v_ref[:, cols].astype(BF16), preferred_element_type=F32)
        o_ref[:, cols] = (o * inv * _silu(g_ref[:, cols])).astype(o_ref.dtype)


MEM_LANE_CHUNKS = MEM_HD // LANES
MEM_ROWS_PER_SLOT = MEM_LANE_CHUNKS * MEM_HEADS


def _mem_head(ref, h, slots):
    return jnp.concatenate(
        [ref[pl.ds(c * MEM_HEADS + h, slots, stride=MEM_ROWS_PER_SLOT), :] for c in range(MEM_LANE_CHUNKS)], axis=1)


def _mem_attn_rows_kernel(q_ref, k_ref, v_ref, g_ref, o_ref):
    slots = k_ref.shape[0] // MEM_ROWS_PER_SLOT
    for h in range(MEM_HEADS):
        cols = slice(h * MEM_HD, (h + 1) * MEM_HD)
        s = _nt_dot(q_ref[:, cols].astype(BF16), _mem_head(k_ref, h, slots).astype(BF16)) * MEM_SCALE
        p = jnp.exp(s - jnp.max(s, axis=-1, keepdims=True))
        inv = 1.0 / jnp.sum(p, axis=-1, keepdims=True)
        o = jnp.dot(p.astype(BF16), _mem_head(v_ref, h, slots).astype(BF16), preferred_element_type=F32)
        o_ref[:, cols] = (o * inv * _silu(g_ref[:, cols])).astype(o_ref.dtype)


def _mem_cache_rows_view(cache):
    depth, nb, slots = cache.shape[:3]
    v = cache.reshape(depth, nb, slots, MEM_HEADS, MEM_LANE_CHUNKS, LANES)
    return jnp.transpose(v, (0, 1, 2, 4, 3, 5)).reshape(depth, nb * slots * MEM_ROWS_PER_SLOT, LANES)


def _mem_attn_sample(h, k_rows, v_rows, layer, batch, slots, tq, row0):
    r0 = row0 // tq
    tok = lambda blk: (lambda b: (r0 + b, blk))
    cache = pl.BlockSpec((None, slots * MEM_ROWS_PER_SLOT, LANES), lambda b: (layer, b, 0))
    return pl.pallas_call(
        _mem_attn_rows_kernel,
        out_shape=jax.ShapeDtypeStruct((batch * tq, MEM_W), F32),
        grid=(batch,),
        in_specs=[pl.BlockSpec((tq, MEM_W), tok(C_MQ // MEM_W)), cache, cache,
                  pl.BlockSpec((tq, MEM_W), tok(C_G_MEM // MEM_W))],
        out_specs=pl.BlockSpec((tq, MEM_W), lambda b: (b, 0)),
        compiler_params=_params("parallel"),
        name="mem_attn_sample",
    )(h, k_rows, v_rows, h)


def _mem_attn(h, k, v, k_spec, v_spec, batch, tq, steps, row0, out_dtype, name):
    r0 = row0 // tq
    tok = lambda blk: (lambda b, i: (r0 + b * steps + i, blk))
    return pl.pallas_call(
        _mem_attn_kernel,
        out_shape=jax.ShapeDtypeStruct((batch * steps * tq, MEM_W), out_dtype),
        grid=(batch, steps),
        in_specs=[pl.BlockSpec((tq, MEM_W), tok(C_MQ // MEM_W)), k_spec, v_spec,
                  pl.BlockSpec((tq, MEM_W), tok(C_G_MEM // MEM_W))],
        out_specs=pl.BlockSpec((tq, MEM_W), lambda b, i: (b * steps + i, 0)),
        compiler_params=_params("parallel", "arbitrary"),
        name=name,
    )(h, k, v, h)


def _fox_cum_sample_kernel(pt_ref, pool_ref, lfn_ref, ct_ref, cq_ref, buf_ref, sem_ref, tri_ref, earlier_ref,
                           *, layer, n_pages, dec_seq):
    b = pl.program_id(0)
    nb = pl.num_programs(0)
    slot = b % 2
    ppb = CUM_BLOCK // PAGE_SIZE
    n_blk = n_pages // ppb
    past = n_pages * PAGE_SIZE

    def page_copy(bb, i, sl):
        return pltpu.make_async_copy(
            pool_ref.at[layer, pt_ref[bb, i]],
            buf_ref.at[sl, i // ppb, :, pl.ds((i % ppb) * PAGE_SIZE, PAGE_SIZE)],
            sem_ref.at[sl])

    def issue(bb, sl):
        for i in range(n_pages):
            page_copy(bb, i, sl).start()

    rows = n_blk * HEAD_ROWS

    @pl.when(b == 0)
    def _():
        issue(0, 0)
        tri_ref[...] = _tri(CUM_BLOCK, upper=True)
        r = lax.broadcasted_iota(jnp.int32, (rows, rows), 0)
        c = lax.broadcasted_iota(jnp.int32, (rows, rows), 1)
        earlier_ref[...] = jnp.where(
            ((r & (HEAD_ROWS - 1)) == (c & (HEAD_ROWS - 1))) & (c < r - (r & (HEAD_ROWS - 1))), 1.0, 0.0).astype(BF16)

    @pl.when(b + 1 < nb)
    def _():
        issue(b + 1, 1 - slot)

    for i in range(n_pages):
        page_copy(b, i, slot).wait()

    x = buf_ref[slot].reshape(rows, CUM_BLOCK)
    p = _prefix_lanes(x, tri_ref[...])
    tot = jnp.broadcast_to(p[:, CUM_BLOCK - 1:], (rows, LANES))
    earlier = earlier_ref[...]
    off = sum(jnp.dot(earlier, t, preferred_element_type=F32) for t in _split3(tot))
    cum = p + off[:, :1]
    for blk in range(n_blk):
        ct_ref[0, :, blk * CUM_BLOCK:(blk + 1) * CUM_BLOCK] = cum[blk * HEAD_ROWS:(blk + 1) * HEAD_ROWS, :]
    p_tot = cum[rows - HEAD_ROWS:, CUM_BLOCK - 1:]

    pad_rows = lambda a: jnp.concatenate([a, jnp.zeros((LANES - a.shape[0], LANES), F32)], axis=0)
    lfn_t = pad_rows(lfn_ref[...]).T[:HEAD_ROWS, :]
    c_new_t = _prefix_lanes(lfn_t, tri_ref[:LANES, :LANES]) + p_tot
    ct_ref[0, :, past:past + LANES] = c_new_t
    c_new = pad_rows(c_new_t).T
    cq_ref[0] = jnp.concatenate(
        [jnp.broadcast_to(c_new[:dec_seq, h:h + 1], (dec_seq, LANES)) for h in range(FOX_HEADS)], axis=0)


def _fox_cum_sample(page_table, pool_t, logf, layer, row0, dec_seq):
    dec_batch, n_pages = page_table.shape
    past = n_pages * PAGE_SIZE
    n_blk = past // CUM_BLOCK
    kern = functools.partial(_fox_cum_sample_kernel, layer=layer, n_pages=n_pages, dec_seq=dec_seq)
    return pl.pallas_call(
        kern,
        out_shape=(jax.ShapeDtypeStruct((dec_batch, HEAD_ROWS, past + LANES), F32),
                   jax.ShapeDtypeStruct((dec_batch, FOX_HEADS * dec_seq, LANES), F32)),
        grid_spec=pltpu.PrefetchScalarGridSpec(
            num_scalar_prefetch=1,
            grid=(dec_batch,),
            in_specs=[pl.BlockSpec(memory_space=pl.ANY),
                      pl.BlockSpec((dec_seq, LANES), lambda b, pt: (row0 // dec_seq + b, 0))],
            out_specs=(pl.BlockSpec((1, HEAD_ROWS, past + LANES), lambda b, pt: (b, 0, 0)),
                       pl.BlockSpec((1, FOX_HEADS * dec_seq, LANES), lambda b, pt: (b, 0, 0))),
            scratch_shapes=[pltpu.VMEM((2, n_blk, HEAD_ROWS, CUM_BLOCK), F32), pltpu.SemaphoreType.DMA((2,)),
                            pltpu.VMEM((CUM_BLOCK, CUM_BLOCK), BF16),
                            pltpu.VMEM((n_blk * HEAD_ROWS, n_blk * HEAD_ROWS), BF16)]),
        compiler_params=_params("arbitrary"),
        name="fox_cum_sample",
    )(page_table, pool_t, logf)


PAGES_PER_CHUNK = 32


def _sample_attn_kernel(pt_ref, q_ref, cos_ref, sin_ref, wuk_ref, wuv_ref, fq_ref,
                        ckvn_ref, krn_ref, fkn_ref, fvn_ref, ct_ref, ctn_ref, cq_ref, gm_ref, gf_ref,
                        ckv_pool, krt_pool, fk_pool, fv_pool,
                        om_ref, of_ref,
                        ckv_buf, krt_buf, fk_buf, fv_buf, sem_ref,
                        qm_ref, qf_ref, m1_ref, l1_ref, a1_ref, m2_ref, l2_ref, a2_ref,
                        *, layer, n_chunks, ppc, dec_seq):
    b = pl.program_id(0)
    j = pl.program_id(1)
    g = b * n_chunks + j
    total = pl.num_programs(0) * n_chunks
    slot = g % 2

    def page_copies(bb, jj, i, sl):
        pid = pt_ref[bb, jj * ppc + i]
        rows = pl.ds(i * PAGE_SIZE, PAGE_SIZE)
        return (pltpu.make_async_copy(ckv_pool.at[layer, pid], ckv_buf.at[sl, rows, :], sem_ref.at[0, sl]),
                pltpu.make_async_copy(krt_pool.at[layer, pid], krt_buf.at[sl, :, rows], sem_ref.at[1, sl]),
                pltpu.make_async_copy(fk_pool.at[layer, pid], fk_buf.at[sl, rows, :], sem_ref.at[2, sl]),
                pltpu.make_async_copy(fv_pool.at[layer, pid], fv_buf.at[sl, rows, :], sem_ref.at[3, sl]))

    def issue(bb, jj, sl):
        for i in range(ppc):
            for cp in page_copies(bb, jj, i, sl):
                cp.start()

    @pl.when(g == 0)
    def _():
        issue(0, 0, 0)

    @pl.when(g + 1 < total)
    def _():
        nxt = g + 1
        issue(nxt // n_chunks, nxt % n_chunks, 1 - slot)

    @pl.when(j == 0)
    def _():
        _build_mla_queries(q_ref, cos_ref[...], sin_ref[...], wuk_ref, qm_ref, dec_seq)
        for h in range(FOX_HEADS):
            qf_ref[h * dec_seq:(h + 1) * dec_seq, :] = fq_ref[:, h * FOX_HD:(h + 1) * FOX_HD].astype(BF16)
        _flash_init(m1_ref, l1_ref, a1_ref)
        _flash_init(m2_ref, l2_ref, a2_ref)

    for i in range(ppc):
        for cp in page_copies(b, j, i, slot):
            cp.wait()

    def fox_logits(fk, ck):
        s = _nt_dot(qf_ref[...], fk) * FOX_SCALE
        cq = cq_ref[0]
        return [(sc + cq) - kc for sc, kc in zip(_lane_chunks(s), _lane_chunks(_head_rows(ck, dec_seq)))]

    ckv = ckv_buf[slot].astype(BF16)
    qm = qm_ref[...]
    s1 = (_nt_dot(qm[:, :KV_LORA], ckv)
          + jnp.dot(qm[:, KV_LORA:], krt_buf[slot].astype(BF16), preferred_element_type=F32)) * MLA_SCALE
    _flash_update(_lane_chunks(s1), ckv, m1_ref, l1_ref, a1_ref)
    _flash_update(fox_logits(fk_buf[slot].astype(BF16), ct_ref[0]), fv_buf[slot].astype(BF16),
                  m2_ref, l2_ref, a2_ref)

    @pl.when(j == n_chunks - 1)
    def _():
        pad = lambda x: jnp.concatenate(
            [x, jnp.zeros((LANES - dec_seq, x.shape[1]), x.dtype)], axis=0).astype(BF16)
        rows = FOX_HEADS * dec_seq
        r = lax.broadcasted_iota(jnp.int32, (rows, LANES), 0)
        c = lax.broadcasted_iota(jnp.int32, (rows, LANES), 1)
        visible = c <= (r & (dec_seq - 1))
        ckv_n = pad(ckvn_ref[...])
        qmm = qm_ref[...]
        s1n = (_nt_dot(qmm[:, :KV_LORA], ckv_n) + _nt_dot(qmm[:, KV_LORA:], pad(krn_ref[...]))) * MLA_SCALE
        _flash_update([jnp.where(visible, s1n, NEG_BIG)], ckv_n, m1_ref, l1_ref, a1_ref)
        s2n, = fox_logits(pad(fkn_ref[...]), ctn_ref[0])
        _flash_update([jnp.where(visible, s2n, NEG_BIG)], pad(fvn_ref[...]), m2_ref, l2_ref, a2_ref)
        _finish_mla(a1_ref, l1_ref, wuv_ref, gm_ref, om_ref, dec_seq)
        _finish_fox(a2_ref, l2_ref, gf_ref, of_ref, dec_seq)


def _sample_attn(page_table, q, cos, sin, wuk, wuv, h, ckv, krope, ct, cq,
                 ckv_pool, krt_pool, fk_pool, fv_pool, layer, row0, dec_seq):
    dec_batch, n_pages = page_table.shape
    ppc = PAGES_PER_CHUNK if n_pages % PAGES_PER_CHUNK == 0 else n_pages
    n_chunks = n_pages // ppc
    tkc = ppc * PAGE_SIZE
    rows = MLA_HEADS * dec_seq
    r0 = row0 // dec_seq
    tok = lambda blk: (lambda b, j, pt: (r0 + b, blk))
    par = lambda b, j, pt: (layer, 0, 0, 0)
    anyspec = pl.BlockSpec(memory_space=pl.ANY)
    kern = functools.partial(_sample_attn_kernel, layer=layer, n_chunks=n_chunks, ppc=ppc, dec_seq=dec_seq)
    return pl.pallas_call(
        kern,
        out_shape=(jax.ShapeDtypeStruct((dec_batch * dec_seq, MLA_W), F32),
                   jax.ShapeDtypeStruct((dec_batch * dec_seq, FOX_W), F32)),
        grid_spec=pltpu.PrefetchScalarGridSpec(
            num_scalar_prefetch=1,
            grid=(dec_batch, n_chunks),
            in_specs=[pl.BlockSpec((dec_seq, q.shape[1]), tok(0)),
                      pl.BlockSpec((dec_seq, QK_ROPE), tok(0)),
                      pl.BlockSpec((dec_seq, QK_ROPE), tok(0)),
                      pl.BlockSpec((None,) + wuk.shape[1:], par),
                      pl.BlockSpec((None,) + wuv.shape[1:], par),
                      pl.BlockSpec((dec_seq, FOX_W), tok(C_FQ // FOX_W)),
                      pl.BlockSpec((dec_seq, KV_LORA), tok(0)),
                      pl.BlockSpec((dec_seq, QK_ROPE), tok(0)),
                      pl.BlockSpec((dec_seq, FOX_HD), tok(C_FK // FOX_HD)),
                      pl.BlockSpec((dec_seq, FOX_HD), tok(C_FV // FOX_HD)),
                      pl.BlockSpec((1, HEAD_ROWS, tkc), lambda b, j, pt: (b, 0, j)),
                      pl.BlockSpec((1, HEAD_ROWS, LANES), lambda b, j, pt: (b, 0, n_pages)),
                      pl.BlockSpec((1, rows, LANES), lambda b, j, pt: (b, 0, 0)),
                      pl.BlockSpec((dec_seq, MLA_W), tok(C_G_MLA // MLA_W)),
                      pl.BlockSpec((dec_seq, FOX_W), tok(C_G_FOX // FOX_W)),
                      anyspec, anyspec, anyspec, anyspec],
            out_specs=(pl.BlockSpec((dec_seq, MLA_W), lambda b, j, pt: (b, 0)),
                       pl.BlockSpec((dec_seq, FOX_W), lambda b, j, pt: (b, 0))),
            scratch_shapes=[pltpu.VMEM((2, tkc, KV_LORA), F32), pltpu.VMEM((2, QK_ROPE, tkc), F32),
                            pltpu.VMEM((2, tkc, FOX_HD), F32), pltpu.VMEM((2, tkc, FOX_HD), F32),
                            pltpu.SemaphoreType.DMA((4, 2)),
                            pltpu.VMEM((rows, MLA_QK), BF16), pltpu.VMEM((rows, FOX_HD), BF16),
                            pltpu.VMEM((rows, LANES), F32), pltpu.VMEM((rows, LANES), F32),
                            pltpu.VMEM((rows, KV_LORA), F32),
                            pltpu.VMEM((rows, LANES), F32), pltpu.VMEM((rows, LANES), F32),
                            pltpu.VMEM((rows, FOX_HD), F32)]),
        compiler_params=_params("arbitrary", "arbitrary"),
        name="sample_attn",
    )(page_table, q, cos, sin, wuk, wuv, h, ckv, krope, h, h, ct, ct, cq, h, h,
      ckv_pool, krt_pool, fk_pool, fv_pool)


def _ln_kernel(x_ref, y_ref, g_ref, b_ref, o_ref, ob_ref, *, alpha):
    v = alpha * x_ref[...] + y_ref[...]
    mu = jnp.mean(v, axis=-1, keepdims=True)
    vc = v - mu
    var = jnp.mean(vc * vc, axis=-1, keepdims=True)
    out = vc * lax.rsqrt(var + LN_EPS) * g_ref[...] + b_ref[...]
    o_ref[...] = out
    ob_ref[...] = out.astype(BF16)


def _layernorm(x, y, g, b, layer, alpha):
    m, d = x.shape
    tm = _pick(m, 256)
    row = lambda i: (i, 0)
    par = lambda i: (layer, 0, 0)
    return pl.pallas_call(
        functools.partial(_ln_kernel, alpha=alpha),
        out_shape=(jax.ShapeDtypeStruct((m, d), F32), jax.ShapeDtypeStruct((m, d), BF16)),
        grid=(m // tm,),
        in_specs=[pl.BlockSpec((tm, d), row), pl.BlockSpec((tm, d), row),
                  pl.BlockSpec((None, 1, d), par), pl.BlockSpec((None, 1, d), par)],
        out_specs=(pl.BlockSpec((tm, d), row), pl.BlockSpec((tm, d), row)),
        compiler_params=_params("parallel"),
        name="residual_layernorm",
    )(x, y, g, b)


def _split_cols(w):
    offs = [0]
    for s in IN_SIZES:
        offs.append(offs[-1] + s)
    return [w[..., offs[i]:offs[i + 1]] for i in range(len(IN_SIZES))]


def _arrange_w_in(w_in):
    q_a, kv_a, k_pe, fq, fk, fv, f_logit, mq, gate = _split_cols(w_in)
    lead = w_in.shape[:-1]
    zeros = lambda n: jnp.zeros(lead + (n,), w_in.dtype)
    misc = jnp.concatenate([f_logit, zeros(LANES - FOX_HEADS - QK_ROPE), k_pe], axis=-1)
    parts = [fq, gate[..., :MLA_W], gate[..., MLA_W:MLA_W + FOX_W], q_a, kv_a, fk, fv, misc, zeros(LANES),
             gate[..., MLA_W + FOX_W:], mq]
    out = jnp.concatenate(parts, axis=-1).astype(BF16)
    assert out.shape[-1] == N_PROJ
    return out


def _arrange_w_q_up(w_q_up):
    depth, lora, _ = w_q_up.shape
    w = w_q_up.reshape(depth, lora, MLA_HEADS, QK_NOPE + QK_ROPE)
    nope = w[..., :QK_NOPE].reshape(depth, lora, MLA_HEADS * QK_NOPE)
    rope = w[..., QK_NOPE:].reshape(depth, lora, MLA_HEADS * QK_ROPE)
    return jnp.concatenate([nope, rope], axis=-1).astype(BF16)


def _rope_tables(positions):
    half = QK_ROPE // 2
    inv = ROPE_THETA ** (-jnp.arange(half, dtype=F32) / half)
    ang = positions.astype(F32)[:, None] * inv[None, :]
    cos, sin = jnp.cos(ang), jnp.sin(ang)
    return jnp.concatenate([cos, cos], axis=-1), jnp.concatenate([sin, sin], axis=-1)


def kernel(x_prompt, x_sample, cache_mla_ckv, cache_mla_krope, cache_fox_k, cache_fox_v, cache_fox_logf,
           cache_mem_k, cache_mem_v, page_table, mem_prompt, w_in, b_f, q_a_norm, w_q_up, kv_a_norm, w_kv_up,
           w_mem_k, w_mem_v, w_out, ln_g, ln_b):
    batch, seq, d_model = x_prompt.shape
    dec_batch, dec_seq, _ = x_sample.shape
    depth = w_in.shape[0]
    n_pages = page_table.shape[1]
    past = n_pages * PAGE_SIZE
    mem_slots = mem_prompt.shape[1]
    assert w_in.shape[-1] == sum(IN_SIZES) and cache_mla_ckv.shape[2] == PAGE_SIZE
    assert seq % PROMPT_TK == 0 and PROMPT_TK % Q_BLOCK == 0 and dec_seq % SUBLANES == 0
    assert past % CUM_BLOCK == 0 and CUM_BLOCK % PAGE_SIZE == 0
    alpha = (2 * depth) ** 0.25
    n_p = batch * seq
    n_s = dec_batch * dec_seq

    w_in_r = _arrange_w_in(w_in)
    w_q_r = _arrange_w_q_up(w_q_up)
    wuk = jnp.transpose(w_kv_up[..., :QK_NOPE], (0, 2, 3, 1)).astype(BF16)
    wuv = jnp.transpose(w_kv_up[..., QK_NOPE:], (0, 2, 1, 3)).astype(BF16)
    w_mem = jnp.concatenate([w_mem_k, w_mem_v], axis=-1).astype(BF16)
    w_out_b = w_out.astype(BF16)
    b_f_pad = jnp.pad(b_f, ((0, 0), (0, LANES - FOX_HEADS)))[:, None, :]
    q_norm, kv_norm = q_a_norm[:, None, :], kv_a_norm[:, None, :]
    ln_g3, ln_b3 = ln_g[:, None, :], ln_b[:, None, :]
    mem_b = mem_prompt.reshape(batch * mem_slots, d_model).astype(BF16)
    krt_pool = jnp.swapaxes(cache_mla_krope, 2, 3)
    lft_pool = jnp.swapaxes(
        jnp.pad(cache_fox_logf, ((0, 0), (0, 0), (0, 0), (0, HEAD_ROWS - FOX_HEADS))), 2, 3)
    mem_k_s = _mem_cache_rows_view(cache_mem_k)
    mem_v_s = _mem_cache_rows_view(cache_mem_v)

    pos = jnp.concatenate([jnp.tile(jnp.arange(seq), batch), jnp.tile(past + jnp.arange(dec_seq), dec_batch)])
    cos, sin = _rope_tables(pos)

    x = jnp.concatenate([x_prompt.reshape(n_p, d_model), x_sample.reshape(n_s, d_model)], axis=0)
    xb = x.astype(BF16)

    tq_mem = _pick(seq, 512)
    outs = {k: [] for k in ("p_ckv", "p_kr", "p_fk", "p_fv", "p_lf", "p_mk", "p_mv",
                            "s_ckv", "s_kr", "s_fk", "s_fv", "s_lf")}
    for l in range(depth):
        h = _linear(xb, w_in_r, l, F32, "proj_in")
        qan, ckv, krope, kmla, fkv, logf = _prep(h, q_norm, kv_norm, b_f_pad, cos, sin, l)
        q = _linear(qan, w_q_r, l, F32, "q_up")

        c, ct = _cumsum_prompt(logf, batch, seq)
        o_mla_p = _mla_prompt(q, cos, sin, wuk, kmla, wuv, h, l, batch, seq)
        o_fox_p = _fox_prompt(h, fkv, c, ct, batch, seq)
        mkv = _linear(mem_b, w_mem, l, F32, "mem_kv")
        o_mem_p = _mem_attn(h, mkv, mkv,
                            pl.BlockSpec((mem_slots, MEM_W), lambda b, i: (b, 0)),
                            pl.BlockSpec((mem_slots, MEM_W), lambda b, i: (b, 1)),
                            batch, tq_mem, seq // tq_mem, 0, BF16, "mem_attn_prompt")

        ct_s, cq_s = _fox_cum_sample(page_table, lft_pool, logf, l, n_p, dec_seq)
        o_mla_s, o_fox_s = _sample_attn(page_table, q, cos, sin, wuk, wuv, h, ckv, krope, ct_s, cq_s,
                                        cache_mla_ckv, krt_pool, cache_fox_k, cache_fox_v, l, n_p, dec_seq)
        o_mem_s = _mem_attn_sample(h, mem_k_s, mem_v_s, l, dec_batch, mem_slots, dec_seq, n_p)
        a_s = jnp.concatenate([o_mla_s, o_fox_s, o_mem_s], axis=1).astype(BF16)

        y = _proj_out(o_mla_p, o_fox_p, o_mem_p, a_s, w_out_b, l)
        x, xb = _layernorm(x, y, ln_g3, ln_b3, l, alpha)

        fk, fv = h[:, C_FK:C_FK + FOX_HD], h[:, C_FV:C_FV + FOX_HD]
        lf = logf[:, :FOX_HEADS]
        outs["p_ckv"].append(ckv[:n_p].reshape(batch, seq, KV_LORA))
        outs["p_kr"].append(krope[:n_p].reshape(batch, seq, QK_ROPE))
        outs["p_fk"].append(fk[:n_p].reshape(batch, seq, FOX_HD))
        outs["p_fv"].append(fv[:n_p].reshape(batch, seq, FOX_HD))
        outs["p_lf"].append(lf[:n_p].reshape(batch, seq, FOX_HEADS))
        outs["p_mk"].append(mkv[:, :MEM_W].reshape(batch, mem_slots, MEM_HEADS, MEM_HD))
        outs["p_mv"].append(mkv[:, MEM_W:].reshape(batch, mem_slots, MEM_HEADS, MEM_HD))
        outs["s_ckv"].append(ckv[n_p:].reshape(dec_batch, dec_seq, KV_LORA))
        outs["s_kr"].append(krope[n_p:].reshape(dec_batch, dec_seq, QK_ROPE))
        outs["s_fk"].append(fk[n_p:].reshape(dec_batch, dec_seq, FOX_HD))
        outs["s_fv"].append(fv[n_p:].reshape(dec_batch, dec_seq, FOX_HD))
        outs["s_lf"].append(lf[n_p:].reshape(dec_batch, dec_seq, FOX_HEADS))

    st = {k: jnp.stack(v) for k, v in outs.items()}
    return (x[:n_p].reshape(batch, seq, d_model), x[n_p:].reshape(dec_batch, dec_seq, d_model),
            st["p_ckv"], st["p_kr"], st["p_fk"], st["p_fv"], st["p_lf"], st["p_mk"], st["p_mv"],
            st["s_ckv"], st["s_kr"], st["s_fk"], st["s_fv"], st["s_lf"])
```

```python
import functools
import math

import jax
import jax.numpy as jnp
from jax import lax
from jax.experimental import pallas as pl
from jax.experimental.pallas import tpu as pltpu

F32 = jnp.float32
BF16 = jnp.bfloat16

MLA_HEADS = 12
Q_LORA = 768
KV_LORA = 256
QK_NOPE = 128
QK_ROPE = 64
MLA_V = 128
ROPE_THETA = 10000.0
MLA_SCALE = (QK_NOPE + QK_ROPE) ** -0.5
FOX_HEADS = 12
FOX_HD = 128
FOX_SCALE = FOX_HD ** -0.5
MEM_HEADS = 4
MEM_HD = 256
MEM_SCALE = MEM_HD ** -0.5
MLA_W = MLA_HEADS * MLA_V
FOX_W = FOX_HEADS * FOX_HD
MEM_W = MEM_HEADS * MEM_HD
MIX_W = MLA_W + FOX_W + MEM_W
IN_SIZES = (Q_LORA, KV_LORA, QK_ROPE, FOX_W, FOX_HD, FOX_HD, FOX_HEADS, MEM_W, MIX_W)
RMS_EPS = 1e-6
LN_EPS = 1e-5
PAGE_SIZE = 128
Q_BLOCK = 128
MLA_QK = KV_LORA + QK_ROPE

LANES = 128
SUBLANES = 8
HEAD_ROWS = 2 * SUBLANES
VMEM_LIMIT = 48 * 1024 * 1024
WEIGHT_TILE_BYTES = 4 * 1024 * 1024
NEG_BIG = -1e30

C_FQ = 0
C_G_MLA = C_FQ + FOX_W
C_G_FOX = C_G_MLA + MLA_W
C_QA = C_G_FOX + FOX_W
C_KVA = C_QA + Q_LORA
C_FK = C_KVA + KV_LORA
C_FV = C_FK + FOX_HD
C_MISC = C_FV + FOX_HD
C_PAD = C_MISC + LANES
C_G_MEM = C_PAD + LANES
C_MQ = C_G_MEM + MEM_W
N_PROJ = C_MQ + MEM_W
MISC_ROPE = LANES - QK_ROPE


def _pick(dim, pref):
    if dim <= pref:
        return dim
    t = (pref // LANES) * LANES
    while t > LANES and dim % t:
        t -= LANES
    return t if dim % t == 0 else dim


def _params(*semantics):
    return pltpu.CompilerParams(dimension_semantics=semantics, vmem_limit_bytes=VMEM_LIMIT)


def _linear_kernel(x_ref, w_ref, o_ref):
    o_ref[...] = jnp.dot(x_ref[...], w_ref[...], preferred_element_type=F32).astype(o_ref.dtype)


def _linear(x, w, layer, out_dtype, name):
    m, k = x.shape
    n = w.shape[-1]
    tm = _pick(m, 1024)
    tn = _pick(n, max(LANES, min(1024, WEIGHT_TILE_BYTES // (2 * k))))
    return pl.pallas_call(
        _linear_kernel,
        out_shape=jax.ShapeDtypeStruct((m, n), out_dtype),
        grid=(m // tm, n // tn),
        in_specs=[pl.BlockSpec((tm, k), lambda i, j: (i, 0)),
                  pl.BlockSpec((None, k, tn), lambda i, j: (layer, 0, j))],
        out_specs=pl.BlockSpec((tm, tn), lambda i, j: (i, j)),
        compiler_params=_params("parallel", "arbitrary"),
        name=name,
    )(x, w)


def _proj_out_kernel(p1_ref, p2_ref, p3_ref, s1_ref, s2_ref, s3_ref, w_ref, o_ref, *, n_prompt_tiles):
    def run(r1, r2, r3):
        dot = lambda a, lo, hi: jnp.dot(a[...], w_ref[lo:hi, :], preferred_element_type=F32)
        o_ref[...] = dot(r1, 0, MLA_W) + dot(r2, MLA_W, MLA_W + FOX_W) + dot(r3, MLA_W + FOX_W, MIX_W)

    i = pl.program_id(0)

    @pl.when(i < n_prompt_tiles)
    def _():
        run(p1_ref, p2_ref, p3_ref)

    @pl.when(i >= n_prompt_tiles)
    def _():
        run(s1_ref, s2_ref, s3_ref)


def _proj_out(o_mla_p, o_fox_p, o_mem_p, a_s, w_out, layer):
    n_p, n_s = o_mla_p.shape[0], a_s.shape[0]
    d = w_out.shape[-1]
    tm = _pick(math.gcd(n_p, n_s), 1024)
    tn = _pick(d, 512)
    npt = n_p // tm
    prow = lambda i, j: (jnp.minimum(i, npt - 1), 0)
    srow = lambda blk: (lambda i, j: (jnp.maximum(i - npt, 0), blk))
    sspec = lambda w, blk: pl.BlockSpec((tm, w), srow(blk), pipeline_mode=pl.Buffered(1))
    return pl.pallas_call(
        functools.partial(_proj_out_kernel, n_prompt_tiles=npt),
        out_shape=jax.ShapeDtypeStruct((n_p + n_s, d), F32),
        grid=((n_p + n_s) // tm, d // tn),
        in_specs=[pl.BlockSpec((tm, MLA_W), prow), pl.BlockSpec((tm, FOX_W), prow), pl.BlockSpec((tm, MEM_W), prow),
                  sspec(MLA_W, 0), sspec(FOX_W, MLA_W // FOX_W), sspec(MEM_W, (MLA_W + FOX_W) // MEM_W),
                  pl.BlockSpec((None, MIX_W, tn), lambda i, j: (layer, 0, j))],
        out_specs=pl.BlockSpec((tm, tn), lambda i, j: (i, j)),
        compiler_params=_params("parallel", "arbitrary"),
        name="proj_out",
    )(o_mla_p, o_fox_p, o_mem_p, a_s, a_s, a_s, w_out)


def _rope(x, cos, sin):
    half = QK_ROPE // 2
    rot = jnp.concatenate([-x[:, half:], x[:, :half]], axis=-1)
    return x * cos + rot * sin


def _rms(x, g):
    return x * lax.rsqrt(jnp.mean(x * x, axis=-1, keepdims=True) + RMS_EPS) * g


def _prep_kernel(qa_ref, kva_ref, fk_ref, fv_ref, misc_ref, qg_ref, kg_ref, bf_ref, cos_ref, sin_ref,
                 qan_ref, ckv_ref, krope_ref, kmla_ref, fkv_ref, logf_ref):
    qan_ref[...] = _rms(qa_ref[...], qg_ref[...]).astype(BF16)
    ckv = _rms(kva_ref[...], kg_ref[...])
    ckv_ref[...] = ckv
    misc = misc_ref[...]
    kr = _rope(misc[:, MISC_ROPE:], cos_ref[...], sin_ref[...])
    krope_ref[...] = kr
    kmla_ref[:, :KV_LORA] = ckv.astype(BF16)
    kmla_ref[:, KV_LORA:] = kr.astype(BF16)
    fkv_ref[:, :FOX_HD] = fk_ref[...].astype(BF16)
    fkv_ref[:, FOX_HD:] = fv_ref[...].astype(BF16)
    z = misc + bf_ref[...]
    lf = -(jnp.maximum(-z, 0.0) + jnp.log1p(jnp.exp(-jnp.abs(z))))
    lane = lax.broadcasted_iota(jnp.int32, lf.shape, 1)
    logf_ref[...] = jnp.where(lane < FOX_HEADS, lf, 0.0)


def _prep(h, q_norm, kv_norm, b_f, cos, sin, layer):
    m = h.shape[0]
    tm = _pick(m, 512)
    row = lambda blk: (lambda i: (i, blk))
    par = lambda i: (layer, 0, 0)
    return pl.pallas_call(
        _prep_kernel,
        out_shape=(jax.ShapeDtypeStruct((m, Q_LORA), BF16),
                   jax.ShapeDtypeStruct((m, KV_LORA), F32),
                   jax.ShapeDtypeStruct((m, QK_ROPE), F32),
                   jax.ShapeDtypeStruct((m, MLA_QK), BF16),
                   jax.ShapeDtypeStruct((m, 2 * FOX_HD), BF16),
                   jax.ShapeDtypeStruct((m, LANES), F32)),
        grid=(m // tm,),
        in_specs=[pl.BlockSpec((tm, Q_LORA), row(C_QA // Q_LORA)),
                  pl.BlockSpec((tm, KV_LORA), row(C_KVA // KV_LORA)),
                  pl.BlockSpec((tm, FOX_HD), row(C_FK // FOX_HD)),
                  pl.BlockSpec((tm, FOX_HD), row(C_FV // FOX_HD)),
                  pl.BlockSpec((tm, LANES), row(C_MISC // LANES)),
                  pl.BlockSpec((None, 1, Q_LORA), par),
                  pl.BlockSpec((None, 1, KV_LORA), par),
                  pl.BlockSpec((None, 1, LANES), par),
                  pl.BlockSpec((tm, QK_ROPE), row(0)),
                  pl.BlockSpec((tm, QK_ROPE), row(0))],
        out_specs=(pl.BlockSpec((tm, Q_LORA), row(0)),
                   pl.BlockSpec((tm, KV_LORA), row(0)),
                   pl.BlockSpec((tm, QK_ROPE), row(0)),
                   pl.BlockSpec((tm, MLA_QK), row(0)),
                   pl.BlockSpec((tm, 2 * FOX_HD), row(0)),
                   pl.BlockSpec((tm, LANES), row(0))),
        compiler_params=_params("parallel"),
        name="prep",
    )(h, h, h, h, h, q_norm, kv_norm, b_f, cos, sin)


CUM_BLOCK = 256


def _tri(n, upper):
    r = lax.broadcasted_iota(jnp.int32, (n, n), 0)
    c = lax.broadcasted_iota(jnp.int32, (n, n), 1)
    return jnp.where((r <= c) if upper else (r >= c), 1.0, 0.0).astype(BF16)


def _split3(x):
    x1 = x.astype(BF16)
    r1 = x - x1.astype(F32)
    x2 = r1.astype(BF16)
    x3 = (r1 - x2.astype(F32)).astype(BF16)
    return x1, x2, x3


def _prefix_rows(tri_lower, x):
    return sum(jnp.dot(tri_lower, t, preferred_element_type=F32) for t in _split3(x))


def _prefix_lanes(x, tri_upper):
    return sum(jnp.dot(t, tri_upper, preferred_element_type=F32) for t in _split3(x))


def _cumsum_prompt_kernel(lf_ref, c_ref, ct_ref):
    t = lf_ref.shape[0]
    tri = _tri(CUM_BLOCK, upper=False)
    carry = jnp.zeros((1, LANES), F32)
    for i in range(t // CUM_BLOCK):
        rows = slice(i * CUM_BLOCK, (i + 1) * CUM_BLOCK)
        c = _prefix_rows(tri, lf_ref[rows, :]) + carry
        c_ref[rows, :] = c
        ct_ref[0, :, rows] = c.T[:HEAD_ROWS, :]
        carry = c[CUM_BLOCK - 1:, :]


def _cumsum_prompt(logf, batch, seq):
    return pl.pallas_call(
        _cumsum_prompt_kernel,
        out_shape=(jax.ShapeDtypeStruct((batch * seq, LANES), F32),
                   jax.ShapeDtypeStruct((batch, HEAD_ROWS, seq), F32)),
        grid=(batch,),
        in_specs=[pl.BlockSpec((seq, LANES), lambda b: (b, 0))],
        out_specs=(pl.BlockSpec((seq, LANES), lambda b: (b, 0)),
                   pl.BlockSpec((1, HEAD_ROWS, seq), lambda b: (b, 0, 0))),
        compiler_params=_params("parallel"),
        name="cumsum_prompt",
    )(logf)


def _lane_chunks(s):
    return [s[:, c * LANES:(c + 1) * LANES] for c in range(s.shape[1] // LANES)]


def _flash_update(chunks, v, m_ref, l_ref, acc_ref):
    m_prev = m_ref[...]
    cmax = functools.reduce(jnp.maximum, chunks)
    m_new = jnp.maximum(m_prev, jnp.max(cmax, axis=-1, keepdims=True))
    alpha = jnp.exp(m_prev - m_new)
    ps = [jnp.exp(c - m_new) for c in chunks]
    l_ref[...] = alpha * l_ref[...] + functools.reduce(jnp.add, ps)
    p = jnp.concatenate([x.astype(BF16) for x in ps], axis=1)
    pv = jnp.dot(p, v, preferred_element_type=F32)
    for c in range(acc_ref.shape[1] // LANES):
        cols = slice(c * LANES, (c + 1) * LANES)
        acc_ref[:, cols] = alpha * acc_ref[:, cols] + pv[:, cols]
    m_ref[...] = m_new


def _flash_init(m_ref, l_ref, acc_ref):
    m_ref[...] = jnp.full(m_ref.shape, NEG_BIG, F32)
    l_ref[...] = jnp.zeros_like(l_ref)
    acc_ref[...] = jnp.zeros_like(acc_ref)


def _row_scale(l_ref):
    return 1.0 / jnp.sum(l_ref[...], axis=-1, keepdims=True)


def _nt_dot(a, b):
    return lax.dot_general(a, b, (((1,), (1,)), ((), ())), preferred_element_type=F32)


def _silu(g):
    return g * (1.0 / (1.0 + jnp.exp(-g)))


def _build_mla_queries(q_ref, cos, sin, wuk_ref, qs_ref, rows):
    heads = range(MLA_HEADS)
    ql = [jnp.dot(q_ref[:, h * QK_NOPE:(h + 1) * QK_NOPE].astype(BF16), wuk_ref[h], preferred_element_type=F32)
          for h in heads]
    for h in heads:
        qs_ref[h * rows:(h + 1) * rows, :KV_LORA] = ql[h].astype(BF16)
        off = MLA_HEADS * QK_NOPE + h * QK_ROPE
        qs_ref[h * rows:(h + 1) * rows, KV_LORA:] = _rope(q_ref[:, off:off + QK_ROPE], cos, sin).astype(BF16)


def _finish_mla(acc_ref, l_ref, wuv_ref, g_ref, o_ref, rows):
    inv = _row_scale(l_ref)
    heads = range(MLA_HEADS)
    rsl = [slice(h * rows, (h + 1) * rows) for h in heads]
    o = [jnp.dot((acc_ref[rsl[h], :] * inv[rsl[h], :]).astype(BF16), wuv_ref[h], preferred_element_type=F32)
         for h in heads]
    for h in heads:
        cols = slice(h * MLA_V, (h + 1) * MLA_V)
        o_ref[:, cols] = (o[h] * _silu(g_ref[:, cols])).astype(o_ref.dtype)


def _finish_fox(acc_ref, l_ref, g_ref, o_ref, rows):
    inv = _row_scale(l_ref)
    for h in range(FOX_HEADS):
        sl = slice(h * rows, (h + 1) * rows)
        cols = slice(h * FOX_HD, (h + 1) * FOX_HD)
        o_ref[:, cols] = (acc_ref[sl, :] * inv[sl, :] * _silu(g_ref[:, cols])).astype(o_ref.dtype)


def _causal_mask(s, q0, k0, rows):
    r = lax.broadcasted_iota(jnp.int32, s.shape, 0)
    c = lax.broadcasted_iota(jnp.int32, s.shape, 1)
    return jnp.where(q0 + (r & (rows - 1)) >= k0 + c, s, NEG_BIG)


def _head_rows(x, rows):
    return jnp.concatenate([jnp.broadcast_to(x[h:h + 1, :], (rows, x.shape[1])) for h in range(FOX_HEADS)], axis=0)


PROMPT_TK = 256


def _causal_sweep(qi, tq, tk, logits, update, scores_first):
    n_full = (qi * tq + 1) // tk

    def body(jj, carry):
        if scores_first:
            a = logits(2 * jj, False)
            b = logits(2 * jj + 1, False)
            update(*a)
            update(*b)
        else:
            update(*logits(2 * jj, False))
            update(*logits(2 * jj + 1, False))
        return carry

    lax.fori_loop(0, n_full // 2, body, 0)

    @pl.when(n_full % 2 == 1)
    def _():
        update(*logits(n_full - 1, False))

    update(*logits(n_full, True))


def _mla_prompt_kernel(q_ref, cos_ref, sin_ref, wuk_ref, k_ref, wuv_ref, g_ref, o_ref,
                       qs_ref, m_ref, l_ref, acc_ref):
    qi = pl.program_id(1)
    tq, tk = Q_BLOCK, PROMPT_TK
    _build_mla_queries(q_ref, cos_ref[...], sin_ref[...], wuk_ref, qs_ref, tq)
    _flash_init(m_ref, l_ref, acc_ref)

    def logits(j, masked):
        k = k_ref[pl.ds(pl.multiple_of(j * tk, tk), tk), :]
        s = _nt_dot(qs_ref[...], k) * MLA_SCALE
        if masked:
            s = _causal_mask(s, qi * tq, j * tk, tq)
        return _lane_chunks(s), k[:, :KV_LORA]

    _causal_sweep(qi, tq, tk, logits, functools.partial(_flash_update, m_ref=m_ref, l_ref=l_ref, acc_ref=acc_ref),
                  scores_first=True)
    _finish_mla(acc_ref, l_ref, wuv_ref, g_ref, o_ref, tq)


def _fox_prompt_kernel(q_ref, kv_ref, c_ref, ct_ref, g_ref, o_ref, qs_ref, cq_ref, m_ref, l_ref, acc_ref):
    qi = pl.program_id(1)
    tq, tk = Q_BLOCK, PROMPT_TK
    for h in range(FOX_HEADS):
        rows = slice(h * tq, (h + 1) * tq)
        qs_ref[rows, :] = q_ref[:, h * FOX_HD:(h + 1) * FOX_HD].astype(BF16)
        cq_ref[rows, :] = jnp.broadcast_to(c_ref[:, h:h + 1], (tq, LANES))
    _flash_init(m_ref, l_ref, acc_ref)

    def logits(j, masked):
        k0 = pl.multiple_of(j * tk, tk)
        kv = kv_ref[pl.ds(k0, tk), :]
        s = _nt_dot(qs_ref[...], kv[:, :FOX_HD]) * FOX_SCALE
        ck = _head_rows(ct_ref[0, :, pl.ds(k0, tk)], tq)
        cq = cq_ref[...]
        chunks = [(sc + cq) - kc for sc, kc in zip(_lane_chunks(s), _lane_chunks(ck))]
        if masked:
            chunks = _lane_chunks(_causal_mask(jnp.concatenate(chunks, axis=1), qi * tq, j * tk, tq))
        return chunks, kv[:, FOX_HD:]

    _causal_sweep(qi, tq, tk, logits, functools.partial(_flash_update, m_ref=m_ref, l_ref=l_ref, acc_ref=acc_ref),
                  scores_first=False)
    _finish_fox(acc_ref, l_ref, g_ref, o_ref, tq)


def _mla_prompt(q, cos, sin, wuk, kmla, wuv, h, layer, batch, seq):
    nq = seq // Q_BLOCK
    rows = MLA_HEADS * Q_BLOCK
    tok = lambda blk: (lambda b, i: (b * nq + i, blk))
    par = lambda b, i: (layer, 0, 0, 0)
    return pl.pallas_call(
        _mla_prompt_kernel,
        out_shape=jax.ShapeDtypeStruct((batch * seq, MLA_W), BF16),
        grid=(batch, nq),
        in_specs=[pl.BlockSpec((Q_BLOCK, q.shape[1]), tok(0)),
                  pl.BlockSpec((Q_BLOCK, QK_ROPE), tok(0)),
                  pl.BlockSpec((Q_BLOCK, QK_ROPE), tok(0)),
                  pl.BlockSpec((None,) + wuk.shape[1:], par),
                  pl.BlockSpec((seq, MLA_QK), lambda b, i: (b, 0)),
                  pl.BlockSpec((None,) + wuv.shape[1:], par),
                  pl.BlockSpec((Q_BLOCK, MLA_W), tok(C_G_MLA // MLA_W))],
        out_specs=pl.BlockSpec((Q_BLOCK, MLA_W), tok(0)),
        scratch_shapes=[pltpu.VMEM((rows, MLA_QK), BF16), pltpu.VMEM((rows, LANES), F32),
                        pltpu.VMEM((rows, LANES), F32), pltpu.VMEM((rows, KV_LORA), F32)],
        compiler_params=_params("parallel", "arbitrary"),
        name="mla_prompt",
    )(q, cos, sin, wuk, kmla, wuv, h)


def _fox_prompt(h, fkv, c, ct, batch, seq):
    nq = seq // Q_BLOCK
    rows = FOX_HEADS * Q_BLOCK
    tok = lambda blk: (lambda b, i: (b * nq + i, blk))
    return pl.pallas_call(
        _fox_prompt_kernel,
        out_shape=jax.ShapeDtypeStruct((batch * seq, FOX_W), BF16),
        grid=(batch, nq),
        in_specs=[pl.BlockSpec((Q_BLOCK, FOX_W), tok(C_FQ // FOX_W)),
                  pl.BlockSpec((seq, 2 * FOX_HD), lambda b, i: (b, 0)),
                  pl.BlockSpec((Q_BLOCK, LANES), tok(0)),
                  pl.BlockSpec((1, HEAD_ROWS, seq), lambda b, i: (b, 0, 0)),
                  pl.BlockSpec((Q_BLOCK, FOX_W), tok(C_G_FOX // FOX_W))],
        out_specs=pl.BlockSpec((Q_BLOCK, FOX_W), tok(0)),
        scratch_shapes=[pltpu.VMEM((rows, FOX_HD), BF16), pltpu.VMEM((rows, LANES), F32),
                        pltpu.VMEM((rows, LANES), F32), pltpu.VMEM((rows, LANES), F32),
                        pltpu.VMEM((rows, FOX_HD), F32)],
        compiler_params=_params("parallel", "arbitrary"),
        name="fox_prompt",
    )(h, fkv, c, ct, h)


def _mem_attn_kernel(q_ref, k_ref, v_ref, g_ref, o_ref):
    heads = range(MEM_HEADS)
    cols = [slice(h * MEM_HD, (h + 1) * MEM_HD) for h in heads]
    s = [_nt_dot(q_ref[:, cols[h]].astype(BF16), k_ref[:, cols[h]].astype(BF16)) * MEM_SCALE for h in heads]
    p = [jnp.exp(s[h] - jnp.max(s[h], axis=-1, keepdims=True)) for h in heads]
    o = [jnp.dot(p[h].astype(BF16), v_ref[:, cols[h]].astype(BF16), preferred_element_type=F32) for h in heads]
    for h in heads:
        inv = 1.0 / jnp.sum(p[h], axis=-1, keepdims=True)
        o_ref[:, cols[h]] = (o[h] * inv * _silu(g_ref[:, cols[h]])).astype(o_ref.dtype)


MEM_LANE_CHUNKS = MEM_HD // LANES
MEM_ROWS_PER_SLOT = MEM_LANE_CHUNKS * MEM_HEADS


def _mem_head(ref, h, slots):
    return jnp.concatenate(
        [ref[pl.ds(c * MEM_HEADS + h, slots, stride=MEM_ROWS_PER_SLOT), :] for c in range(MEM_LANE_CHUNKS)], axis=1)


def _mem_attn_rows_kernel(q_ref, k_ref, v_ref, g_ref, o_ref):
    slots = k_ref.shape[0] // MEM_ROWS_PER_SLOT
    heads = range(MEM_HEADS)
    cols = [slice(h * MEM_HD, (h + 1) * MEM_HD) for h in heads]
    s = [_nt_dot(q_ref[:, cols[h]].astype(BF16), _mem_head(k_ref, h, slots).astype(BF16)) * MEM_SCALE for h in heads]
    p = [jnp.exp(s[h] - jnp.max(s[h], axis=-1, keepdims=True)) for h in heads]
    o = [jnp.dot(p[h].astype(BF16), _mem_head(v_ref, h, slots).astype(BF16), preferred_element_type=F32)
         for h in heads]
    for h in heads:
        inv = 1.0 / jnp.sum(p[h], axis=-1, keepdims=True)
        o_ref[:, cols[h]] = (o[h] * inv * _silu(g_ref[:, cols[h]])).astype(o_ref.dtype)


def _mem_cache_rows_view(cache):
    depth, nb, slots = cache.shape[:3]
    v = cache.reshape(depth, nb, slots, MEM_HEADS, MEM_LANE_CHUNKS, LANES)
    return jnp.transpose(v, (0, 1, 2, 4, 3, 5)).reshape(depth, nb * slots * MEM_ROWS_PER_SLOT, LANES)


def _mem_attn_sample(h, k_rows, v_rows, layer, batch, slots, tq, row0):
    r0 = row0 // tq
    tok = lambda blk: (lambda b: (r0 + b, blk))
    cache = pl.BlockSpec((None, slots * MEM_ROWS_PER_SLOT, LANES), lambda b: (layer, b, 0))
    return pl.pallas_call(
        _mem_attn_rows_kernel,
        out_shape=jax.ShapeDtypeStruct((batch * tq, MEM_W), F32),
        grid=(batch,),
        in_specs=[pl.BlockSpec((tq, MEM_W), tok(C_MQ // MEM_W)), cache, cache,
                  pl.BlockSpec((tq, MEM_W), tok(C_G_MEM // MEM_W))],
        out_specs=pl.BlockSpec((tq, MEM_W), lambda b: (b, 0)),
        compiler_params=_params("parallel"),
        name="mem_attn_sample",
    )(h, k_rows, v_rows, h)


def _mem_attn(h, k, v, k_spec, v_spec, batch, tq, steps, row0, out_dtype, name):
    r0 = row0 // tq
    tok = lambda blk: (lambda b, i: (r0 + b * steps + i, blk))
    return pl.pallas_call(
        _mem_attn_kernel,
        out_shape=jax.ShapeDtypeStruct((batch * steps * tq, MEM_W), out_dtype),
        grid=(batch, steps),
        in_specs=[pl.BlockSpec((tq, MEM_W), tok(C_MQ // MEM_W)), k_spec, v_spec,
                  pl.BlockSpec((tq, MEM_W), tok(C_G_MEM // MEM_W))],
        out_specs=pl.BlockSpec((tq, MEM_W), lambda b, i: (b * steps + i, 0)),
        compiler_params=_params("parallel", "arbitrary"),
        name=name,
    )(h, k, v, h)


def _fox_cum_sample_kernel(pt_ref, pool_ref, lfn_ref, ct_ref, cq_ref, buf_ref, sem_ref, tri_ref, earlier_ref,
                           *, layer, n_pages, dec_seq):
    b = pl.program_id(0)
    nb = pl.num_programs(0)
    slot = b % 2
    ppb = CUM_BLOCK // PAGE_SIZE
    n_blk = n_pages // ppb
    past = n_pages * PAGE_SIZE

    def page_copy(bb, i, sl):
        return pltpu.make_async_copy(
            pool_ref.at[layer, pt_ref[bb, i]],
            buf_ref.at[sl, i // ppb, :, pl.ds((i % ppb) * PAGE_SIZE, PAGE_SIZE)],
            sem_ref.at[sl])

    def issue(bb, sl):
        for i in range(n_pages):
            page_copy(bb, i, sl).start()

    rows = n_blk * HEAD_ROWS

    @pl.when(b == 0)
    def _():
        issue(0, 0)
        tri_ref[...] = _tri(CUM_BLOCK, upper=True)
        r = lax.broadcasted_iota(jnp.int32, (rows, rows), 0)
        c = lax.broadcasted_iota(jnp.int32, (rows, rows), 1)
        earlier_ref[...] = jnp.where(
            ((r & (HEAD_ROWS - 1)) == (c & (HEAD_ROWS - 1))) & (c < r - (r & (HEAD_ROWS - 1))), 1.0, 0.0).astype(BF16)

    @pl.when(b + 1 < nb)
    def _():
        issue(b + 1, 1 - slot)

    for i in range(n_pages):
        page_copy(b, i, slot).wait()

    x = buf_ref[slot].reshape(rows, CUM_BLOCK)
    p = _prefix_lanes(x, tri_ref[...])
    tot = jnp.broadcast_to(p[:, CUM_BLOCK - 1:], (rows, LANES))
    earlier = earlier_ref[...]
    off = sum(jnp.dot(earlier, t, preferred_element_type=F32) for t in _split3(tot))
    cum = p + off[:, :1]
    for blk in range(n_blk):
        ct_ref[0, :, blk * CUM_BLOCK:(blk + 1) * CUM_BLOCK] = cum[blk * HEAD_ROWS:(blk + 1) * HEAD_ROWS, :]
    p_tot = cum[rows - HEAD_ROWS:, CUM_BLOCK - 1:]

    pad_rows = lambda a: jnp.concatenate([a, jnp.zeros((LANES - a.shape[0], LANES), F32)], axis=0)
    lfn_t = pad_rows(lfn_ref[...]).T[:HEAD_ROWS, :]
    c_new_t = _prefix_lanes(lfn_t, tri_ref[:LANES, :LANES]) + p_tot
    ct_ref[0, :, past:past + LANES] = c_new_t
    c_new = pad_rows(c_new_t).T
    cq_ref[0] = jnp.concatenate(
        [jnp.broadcast_to(c_new[:dec_seq, h:h + 1], (dec_seq, LANES)) for h in range(FOX_HEADS)], axis=0)


def _fox_cum_sample(page_table, pool_t, logf, layer, row0, dec_seq):
    dec_batch, n_pages = page_table.shape
    past = n_pages * PAGE_SIZE
    n_blk = past // CUM_BLOCK
    kern = functools.partial(_fox_cum_sample_kernel, layer=layer, n_pages=n_pages, dec_seq=dec_seq)
    return pl.pallas_call(
        kern,
        out_shape=(jax.ShapeDtypeStruct((dec_batch, HEAD_ROWS, past + LANES), F32),
                   jax.ShapeDtypeStruct((dec_batch, FOX_HEADS * dec_seq, LANES), F32)),
        grid_spec=pltpu.PrefetchScalarGridSpec(
            num_scalar_prefetch=1,
            grid=(dec_batch,),
            in_specs=[pl.BlockSpec(memory_space=pl.ANY),
                      pl.BlockSpec((dec_seq, LANES), lambda b, pt: (row0 // dec_seq + b, 0))],
            out_specs=(pl.BlockSpec((1, HEAD_ROWS, past + LANES), lambda b, pt: (b, 0, 0)),
                       pl.BlockSpec((1, FOX_HEADS * dec_seq, LANES), lambda b, pt: (b, 0, 0))),
            scratch_shapes=[pltpu.VMEM((2, n_blk, HEAD_ROWS, CUM_BLOCK), F32), pltpu.SemaphoreType.DMA((2,)),
                            pltpu.VMEM((CUM_BLOCK, CUM_BLOCK), BF16),
                            pltpu.VMEM((n_blk * HEAD_ROWS, n_blk * HEAD_ROWS), BF16)]),
        compiler_params=_params("arbitrary"),
        name="fox_cum_sample",
    )(page_table, pool_t, logf)


PAGES_PER_CHUNK = 32


def _sample_attn_kernel(pt_ref, q_ref, cos_ref, sin_ref, wuk_ref, wuv_ref, fq_ref,
                        ckvn_ref, krn_ref, fkn_ref, fvn_ref, ct_ref, ctn_ref, cq_ref, gm_ref, gf_ref,
                        ckv_pool, krt_pool, fk_pool, fv_pool,
                        om_ref, of_ref,
                        ckv_buf, krt_buf, fk_buf, fv_buf, sem_ref,
                        qm_ref, qf_ref, m1_ref, l1_ref, a1_ref, m2_ref, l2_ref, a2_ref,
                        *, layer, n_chunks, ppc, dec_seq):
    b = pl.program_id(0)
    j = pl.program_id(1)
    g = b * n_chunks + j
    total = pl.num_programs(0) * n_chunks
    slot = g % 2

    def page_copies(bb, jj, i, sl):
        pid = pt_ref[bb, jj * ppc + i]
        rows = pl.ds(i * PAGE_SIZE, PAGE_SIZE)
        return (pltpu.make_async_copy(ckv_pool.at[layer, pid], ckv_buf.at[sl, rows, :], sem_ref.at[0, sl]),
                pltpu.make_async_copy(krt_pool.at[layer, pid], krt_buf.at[sl, :, rows], sem_ref.at[1, sl]),
                pltpu.make_async_copy(fk_pool.at[layer, pid], fk_buf.at[sl, rows, :], sem_ref.at[2, sl]),
                pltpu.make_async_copy(fv_pool.at[layer, pid], fv_buf.at[sl, rows, :], sem_ref.at[3, sl]))

    def issue(bb, jj, sl):
        for i in range(ppc):
            for cp in page_copies(bb, jj, i, sl):
                cp.start()

    @pl.when(g == 0)
    def _():
        issue(0, 0, 0)

    @pl.when(g + 1 < total)
    def _():
        nxt = g + 1
        issue(nxt // n_chunks, nxt % n_chunks, 1 - slot)

    @pl.when(j == 0)
    def _():
        _build_mla_queries(q_ref, cos_ref[...], sin_ref[...], wuk_ref, qm_ref, dec_seq)
        for h in range(FOX_HEADS):
            qf_ref[h * dec_seq:(h + 1) * dec_seq, :] = fq_ref[:, h * FOX_HD:(h + 1) * FOX_HD].astype(BF16)
        _flash_init(m1_ref, l1_ref, a1_ref)
        _flash_init(m2_ref, l2_ref, a2_ref)

    for i in range(ppc):
        for cp in page_copies(b, j, i, slot):
            cp.wait()

    def fox_logits(fk, ck):
        s = _nt_dot(qf_ref[...], fk) * FOX_SCALE
        cq = cq_ref[0]
        return [(sc + cq) - kc for sc, kc in zip(_lane_chunks(s), _lane_chunks(_head_rows(ck, dec_seq)))]

    ckv = ckv_buf[slot].astype(BF16)
    qm = qm_ref[...]
    s1 = (_nt_dot(qm[:, :KV_LORA], ckv)
          + jnp.dot(qm[:, KV_LORA:], krt_buf[slot].astype(BF16), preferred_element_type=F32)) * MLA_SCALE
    _flash_update(_lane_chunks(s1), ckv, m1_ref, l1_ref, a1_ref)
    _flash_update(fox_logits(fk_buf[slot].astype(BF16), ct_ref[0]), fv_buf[slot].astype(BF16),
                  m2_ref, l2_ref, a2_ref)

    @pl.when(j == n_chunks - 1)
    def _():
        pad = lambda x: jnp.concatenate(
            [x, jnp.zeros((LANES - dec_seq, x.shape[1]), x.dtype)], axis=0).astype(BF16)
        rows = FOX_HEADS * dec_seq
        r = lax.broadcasted_iota(jnp.int32, (rows, LANES), 0)
        c = lax.broadcasted_iota(jnp.int32, (rows, LANES), 1)
        visible = c <= (r & (dec_seq - 1))
        ckv_n = pad(ckvn_ref[...])
        qmm = qm_ref[...]
        s1n = (_nt_dot(qmm[:, :KV_LORA], ckv_n) + _nt_dot(qmm[:, KV_LORA:], pad(krn_ref[...]))) * MLA_SCALE
        _flash_update([jnp.where(visible, s1n, NEG_BIG)], ckv_n, m1_ref, l1_ref, a1_ref)
        s2n, = fox_logits(pad(fkn_ref[...]), ctn_ref[0])
        _flash_update([jnp.where(visible, s2n, NEG_BIG)], pad(fvn_ref[...]), m2_ref, l2_ref, a2_ref)
        _finish_mla(a1_ref, l1_ref, wuv_ref, gm_ref, om_ref, dec_seq)
        _finish_fox(a2_ref, l2_ref, gf_ref, of_ref, dec_seq)


def _sample_attn(page_table, q, cos, sin, wuk, wuv, h, ckv, krope, ct, cq,
                 ckv_pool, krt_pool, fk_pool, fv_pool, layer, row0, dec_seq):
    dec_batch, n_pages = page_table.shape
    ppc = PAGES_PER_CHUNK if n_pages % PAGES_PER_CHUNK == 0 else n_pages
    n_chunks = n_pages // ppc
    tkc = ppc * PAGE_SIZE
    rows = MLA_HEADS * dec_seq
    r0 = row0 // dec_seq
    tok = lambda blk: (lambda b, j, pt: (r0 + b, blk))
    par = lambda b, j, pt: (layer, 0, 0, 0)
    anyspec = pl.BlockSpec(memory_space=pl.ANY)
    kern = functools.partial(_sample_attn_kernel, layer=layer, n_chunks=n_chunks, ppc=ppc, dec_seq=dec_seq)
    return pl.pallas_call(
        kern,
        out_shape=(jax.ShapeDtypeStruct((dec_batch * dec_seq, MLA_W), F32),
                   jax.ShapeDtypeStruct((dec_batch * dec_seq, FOX_W), F32)),
        grid_spec=pltpu.PrefetchScalarGridSpec(
            num_scalar_prefetch=1,
            grid=(dec_batch, n_chunks),
            in_specs=[pl.BlockSpec((dec_seq, q.shape[1]), tok(0)),
                      pl.BlockSpec((dec_seq, QK_ROPE), tok(0)),
                      pl.BlockSpec((dec_seq, QK_ROPE), tok(0)),
                      pl.BlockSpec((None,) + wuk.shape[1:], par),
                      pl.BlockSpec((None,) + wuv.shape[1:], par),
                      pl.BlockSpec((dec_seq, FOX_W), tok(C_FQ // FOX_W)),
                      pl.BlockSpec((dec_seq, KV_LORA), tok(0)),
                      pl.BlockSpec((dec_seq, QK_ROPE), tok(0)),
                      pl.BlockSpec((dec_seq, FOX_HD), tok(C_FK // FOX_HD)),
                      pl.BlockSpec((dec_seq, FOX_HD), tok(C_FV // FOX_HD)),
                      pl.BlockSpec((1, HEAD_ROWS, tkc), lambda b, j, pt: (b, 0, j)),
                      pl.BlockSpec((1, HEAD_ROWS, LANES), lambda b, j, pt: (b, 0, n_pages)),
                      pl.BlockSpec((1, rows, LANES), lambda b, j, pt: (b, 0, 0)),
                      pl.BlockSpec((dec_seq, MLA_W), tok(C_G_MLA // MLA_W)),
                      pl.BlockSpec((dec_seq, FOX_W), tok(C_G_FOX // FOX_W)),
                      anyspec, anyspec, anyspec, anyspec],
            out_specs=(pl.BlockSpec((dec_seq, MLA_W), lambda b, j, pt: (b, 0)),
                       pl.BlockSpec((dec_seq, FOX_W), lambda b, j, pt: (b, 0))),
            scratch_shapes=[pltpu.VMEM((2, tkc, KV_LORA), F32), pltpu.VMEM((2, QK_ROPE, tkc), F32),
                            pltpu.VMEM((2, tkc, FOX_HD), F32), pltpu.VMEM((2, tkc, FOX_HD), F32),
                            pltpu.SemaphoreType.DMA((4, 2)),
                            pltpu.VMEM((rows, MLA_QK), BF16), pltpu.VMEM((rows, FOX_HD), BF16),
                            pltpu.VMEM((rows, LANES), F32), pltpu.VMEM((rows, LANES), F32),
                            pltpu.VMEM((rows, KV_LORA), F32),
                            pltpu.VMEM((rows, LANES), F32), pltpu.VMEM((rows, LANES), F32),
                            pltpu.VMEM((rows, FOX_HD), F32)]),
        compiler_params=_params("arbitrary", "arbitrary"),
        name="sample_attn",
    )(page_table, q, cos, sin, wuk, wuv, h, ckv, krope, h, h, ct, ct, cq, h, h,
      ckv_pool, krt_pool, fk_pool, fv_pool)


def _ln_kernel(x_ref, y_ref, g_ref, b_ref, o_ref, ob_ref, *, alpha):
    v = alpha * x_ref[...] + y_ref[...]
    mu = jnp.mean(v, axis=-1, keepdims=True)
    vc = v - mu
    var = jnp.mean(vc * vc, axis=-1, keepdims=True)
    out = vc * lax.rsqrt(var + LN_EPS) * g_ref[...] + b_ref[...]
    o_ref[...] = out
    ob_ref[...] = out.astype(BF16)


def _layernorm(x, y, g, b, layer, alpha):
    m, d = x.shape
    tm = _pick(m, 256)
    row = lambda i: (i, 0)
    par = lambda i: (layer, 0, 0)
    return pl.pallas_call(
        functools.partial(_ln_kernel, alpha=alpha),
        out_shape=(jax.ShapeDtypeStruct((m, d), F32), jax.ShapeDtypeStruct((m, d), BF16)),
        grid=(m // tm,),
        in_specs=[pl.BlockSpec((tm, d), row), pl.BlockSpec((tm, d), row),
                  pl.BlockSpec((None, 1, d), par), pl.BlockSpec((None, 1, d), par)],
        out_specs=(pl.BlockSpec((tm, d), row), pl.BlockSpec((tm, d), row)),
        compiler_params=_params("parallel"),
        name="residual_layernorm",
    )(x, y, g, b)


def _split_cols(w):
    offs = [0]
    for s in IN_SIZES:
        offs.append(offs[-1] + s)
    return [w[..., offs[i]:offs[i + 1]] for i in range(len(IN_SIZES))]


def _arrange_w_in(w_in):
    q_a, kv_a, k_pe, fq, fk, fv, f_logit, mq, gate = _split_cols(w_in)
    lead = w_in.shape[:-1]
    zeros = lambda n: jnp.zeros(lead + (n,), w_in.dtype)
    misc = jnp.concatenate([f_logit, zeros(LANES - FOX_HEADS - QK_ROPE), k_pe], axis=-1)
    parts = [fq, gate[..., :MLA_W], gate[..., MLA_W:MLA_W + FOX_W], q_a, kv_a, fk, fv, misc, zeros(LANES),
             gate[..., MLA_W + FOX_W:], mq]
    out = jnp.concatenate(parts, axis=-1).astype(BF16)
    assert out.shape[-1] == N_PROJ
    return out


def _arrange_w_q_up(w_q_up):
    depth, lora, _ = w_q_up.shape
    w = w_q_up.reshape(depth, lora, MLA_HEADS, QK_NOPE + QK_ROPE)
    nope = w[..., :QK_NOPE].reshape(depth, lora, MLA_HEADS * QK_NOPE)
    rope = w[..., QK_NOPE:].reshape(depth, lora, MLA_HEADS * QK_ROPE)
    return jnp.concatenate([nope, rope], axis=-1).astype(BF16)


def _rope_tables(positions):
    half = QK_ROPE // 2
    inv = ROPE_THETA ** (-jnp.arange(half, dtype=F32) / half)
    ang = positions.astype(F32)[:, None] * inv[None, :]
    cos, sin = jnp.cos(ang), jnp.sin(ang)
    return jnp.concatenate([cos, cos], axis=-1), jnp.concatenate([sin, sin], axis=-1)


def kernel(x_prompt, x_sample, cache_mla_ckv, cache_mla_krope, cache_fox_k, cache_fox_v, cache_fox_logf,
           cache_mem_k, cache_mem_v, page_table, mem_prompt, w_in, b_f, q_a_norm, w_q_up, kv_a_norm, w_kv_up,
           w_mem_k, w_mem_v, w_out, ln_g, ln_b):
    batch, seq, d_model = x_prompt.shape
    dec_batch, dec_seq, _ = x_sample.shape
    depth = w_in.shape[0]
    n_pages = page_table.shape[1]
    past = n_pages * PAGE_SIZE
    mem_slots = mem_prompt.shape[1]
    assert w_in.shape[-1] == sum(IN_SIZES) and cache_mla_ckv.shape[2] == PAGE_SIZE
    assert seq % PROMPT_TK == 0 and PROMPT_TK % Q_BLOCK == 0 and dec_seq % SUBLANES == 0
    assert past % CUM_BLOCK == 0 and CUM_BLOCK % PAGE_SIZE == 0
    alpha = (2 * depth) ** 0.25
    n_p = batch * seq
    n_s = dec_batch * dec_seq

    w_in_r = _arrange_w_in(w_in)
    w_q_r = _arrange_w_q_up(w_q_up)
    wuk = jnp.transpose(w_kv_up[..., :QK_NOPE], (0, 2, 3, 1)).astype(BF16)
    wuv = jnp.transpose(w_kv_up[..., QK_NOPE:], (0, 2, 1, 3)).astype(BF16)
    w_mem = jnp.concatenate([w_mem_k, w_mem_v], axis=-1).astype(BF16)
    w_out_b = w_out.astype(BF16)
    b_f_pad = jnp.pad(b_f, ((0, 0), (0, LANES - FOX_HEADS)))[:, None, :]
    q_norm, kv_norm = q_a_norm[:, None, :], kv_a_norm[:, None, :]
    ln_g3, ln_b3 = ln_g[:, None, :], ln_b[:, None, :]
    mem_b = mem_prompt.reshape(batch * mem_slots, d_model).astype(BF16)
    krt_pool = jnp.swapaxes(cache_mla_krope, 2, 3)
    lft_pool = jnp.swapaxes(
        jnp.pad(cache_fox_logf, ((0, 0), (0, 0), (0, 0), (0, HEAD_ROWS - FOX_HEADS))), 2, 3)
    mem_k_s = _mem_cache_rows_view(cache_mem_k)
    mem_v_s = _mem_cache_rows_view(cache_mem_v)

    pos = jnp.concatenate([jnp.tile(jnp.arange(seq), batch), jnp.tile(past + jnp.arange(dec_seq), dec_batch)])
    cos, sin = _rope_tables(pos)

    x = jnp.concatenate([x_prompt.reshape(n_p, d_model), x_sample.reshape(n_s, d_model)], axis=0)
    xb = x.astype(BF16)

    tq_mem = _pick(seq, 512)
    outs = {k: [] for k in ("p_ckv", "p_kr", "p_fk", "p_fv", "p_lf", "p_mk", "p_mv",
                            "s_ckv", "s_kr", "s_fk", "s_fv", "s_lf")}
    for l in range(depth):
        h = _linear(xb, w_in_r, l, F32, "proj_in")
        qan, ckv, krope, kmla, fkv, logf = _prep(h, q_norm, kv_norm, b_f_pad, cos, sin, l)
        q = _linear(qan, w_q_r, l, F32, "q_up")

        c, ct = _cumsum_prompt(logf, batch, seq)
        o_mla_p = _mla_prompt(q, cos, sin, wuk, kmla, wuv, h, l, batch, seq)
        o_fox_p = _fox_prompt(h, fkv, c, ct, batch, seq)
        mkv = _linear(mem_b, w_mem, l, F32, "mem_kv")
        o_mem_p = _mem_attn(h, mkv, mkv,
                            pl.BlockSpec((mem_slots, MEM_W), lambda b, i: (b, 0)),
                            pl.BlockSpec((mem_slots, MEM_W), lambda b, i: (b, 1)),
                            batch, tq_mem, seq // tq_mem, 0, BF16, "mem_attn_prompt")

        ct_s, cq_s = _fox_cum_sample(page_table, lft_pool, logf, l, n_p, dec_seq)
        o_mla_s, o_fox_s = _sample_attn(page_table, q, cos, sin, wuk, wuv, h, ckv, krope, ct_s, cq_s,
                                        cache_mla_ckv, krt_pool, cache_fox_k, cache_fox_v, l, n_p, dec_seq)
        o_mem_s = _mem_attn_sample(h, mem_k_s, mem_v_s, l, dec_batch, mem_slots, dec_seq, n_p)
        a_s = jnp.concatenate([o_mla_s, o_fox_s, o_mem_s], axis=1).astype(BF16)

        y = _proj_out(o_mla_p, o_fox_p, o_mem_p, a_s, w_out_b, l)
        x, xb = _layernorm(x, y, ln_g3, ln_b3, l, alpha)

        fk, fv = h[:, C_FK:C_FK + FOX_HD], h[:, C_FV:C_FV + FOX_HD]
        lf = logf[:, :FOX_HEADS]
        outs["p_ckv"].append(ckv[:n_p].reshape(batch, seq, KV_LORA))
        outs["p_kr"].append(krope[:n_p].reshape(batch, seq, QK_ROPE))
        outs["p_fk"].append(fk[:n_p].reshape(batch, seq, FOX_HD))
        outs["p_fv"].append(fv[:n_p].reshape(batch, seq, FOX_HD))
        outs["p_lf"].append(lf[:n_p].reshape(batch, seq, FOX_HEADS))
        outs["p_mk"].append(mkv[:, :MEM_W].reshape(batch, mem_slots, MEM_HEADS, MEM_HD))
        outs["p_mv"].append(mkv[:, MEM_W:].reshape(batch, mem_slots, MEM_HEADS, MEM_HD))
        outs["s_ckv"].append(ckv[n_p:].reshape(dec_batch, dec_seq, KV_LORA))
        outs["s_kr"].append(krope[n_p:].reshape(dec_batch, dec_seq, QK_ROPE))
        outs["s_fk"].append(fk[n_p:].reshape(dec_batch, dec_seq, FOX_HD))
        outs["s_fv"].append(fv[n_p:].reshape(dec_batch, dec_seq, FOX_HD))
        outs["s_lf"].append(lf[n_p:].reshape(dec_batch, dec_seq, FOX_HEADS))

    st = {k: jnp.stack(v) for k, v in outs.items()}
    return (x[:n_p].reshape(batch, seq, d_model), x[n_p:].reshape(dec_batch, dec_seq, d_model),
            st["p_ckv"], st["p_kr"], st["p_fk"], st["p_fv"], st["p_lf"], st["p_mk"], st["p_mv"],
            st["s_ckv"], st["s_kr"], st["s_fk"], st["s_fv"], st["s_lf"])
```

```python
import functools
import math

import jax
import jax.numpy as jnp
from jax import lax
from jax.experimental import pallas as pl
from jax.experimental.pallas import tpu as pltpu

F32 = jnp.float32
BF16 = jnp.bfloat16

MLA_HEADS = 12
Q_LORA = 768
KV_LORA = 256
QK_NOPE = 128
QK_ROPE = 64
MLA_V = 128
ROPE_THETA = 10000.0
MLA_SCALE = (QK_NOPE + QK_ROPE) ** -0.5
FOX_HEADS = 12
FOX_HD = 128
FOX_SCALE = FOX_HD ** -0.5
MEM_HEADS = 4
MEM_HD = 256
MEM_SCALE = MEM_HD ** -0.5
MLA_W = MLA_HEADS * MLA_V
FOX_W = FOX_HEADS * FOX_HD
MEM_W = MEM_HEADS * MEM_HD
MIX_W = MLA_W + FOX_W + MEM_W
IN_SIZES = (Q_LORA, KV_LORA, QK_ROPE, FOX_W, FOX_HD, FOX_HD, FOX_HEADS, MEM_W, MIX_W)
RMS_EPS = 1e-6
LN_EPS = 1e-5
PAGE_SIZE = 128
Q_BLOCK = 128
MLA_QK = KV_LORA + QK_ROPE

LANES = 128
SUBLANES = 8
HEAD_ROWS = 2 * SUBLANES
VMEM_LIMIT = 56 * 1024 * 1024
WEIGHT_TILE_BYTES = 4 * 1024 * 1024
NEG_BIG = -1e30

C_FQ = 0
C_G_MLA = C_FQ + FOX_W
C_G_FOX = C_G_MLA + MLA_W
C_QA = C_G_FOX + FOX_W
C_KVA = C_QA + Q_LORA
C_FK = C_KVA + KV_LORA
C_FV = C_FK + FOX_HD
C_MISC = C_FV + FOX_HD
C_PAD = C_MISC + LANES
C_G_MEM = C_PAD + LANES
C_MQ = C_G_MEM + MEM_W
N_PROJ = C_MQ + MEM_W
MISC_ROPE = LANES - QK_ROPE


def _pick(dim, pref):
    if dim <= pref:
        return dim
    t = (pref // LANES) * LANES
    while t > LANES and dim % t:
        t -= LANES
    return t if dim % t == 0 else dim


def _params(*semantics):
    return pltpu.CompilerParams(dimension_semantics=semantics, vmem_limit_bytes=VMEM_LIMIT)


def _linear_kernel(x_ref, w_ref, o_ref):
    o_ref[...] = jnp.dot(x_ref[...], w_ref[...], preferred_element_type=F32).astype(o_ref.dtype)


def _linear(x, w, layer, out_dtype, name):
    m, k = x.shape
    n = w.shape[-1]
    tm = _pick(m, 1024)
    tn = _pick(n, max(LANES, min(1024, WEIGHT_TILE_BYTES // (2 * k))))
    return pl.pallas_call(
        _linear_kernel,
        out_shape=jax.ShapeDtypeStruct((m, n), out_dtype),
        grid=(m // tm, n // tn),
        in_specs=[pl.BlockSpec((tm, k), lambda i, j: (i, 0)),
                  pl.BlockSpec((None, k, tn), lambda i, j: (layer, 0, j))],
        out_specs=pl.BlockSpec((tm, tn), lambda i, j: (i, j)),
        compiler_params=_params("parallel", "arbitrary"),
        name=name,
    )(x, w)


def _proj_out_kernel(p1_ref, p2_ref, p3_ref, s1_ref, s2_ref, s3_ref, w_ref, o_ref, *, n_prompt_tiles):
    def run(r1, r2, r3):
        dot = lambda a, lo, hi: jnp.dot(a[...], w_ref[lo:hi, :], preferred_element_type=F32)
        o_ref[...] = dot(r1, 0, MLA_W) + dot(r2, MLA_W, MLA_W + FOX_W) + dot(r3, MLA_W + FOX_W, MIX_W)

    i = pl.program_id(0)

    @pl.when(i < n_prompt_tiles)
    def _():
        run(p1_ref, p2_ref, p3_ref)

    @pl.when(i >= n_prompt_tiles)
    def _():
        run(s1_ref, s2_ref, s3_ref)


def _proj_out(o_mla_p, o_fox_p, o_mem_p, a_s, w_out, layer):
    n_p, n_s = o_mla_p.shape[0], a_s.shape[0]
    d = w_out.shape[-1]
    tm = _pick(math.gcd(n_p, n_s), 1024)
    tn = _pick(d, 512)
    npt = n_p // tm
    prow = lambda i, j: (jnp.minimum(i, npt - 1), 0)
    srow = lambda blk: (lambda i, j: (jnp.maximum(i - npt, 0), blk))
    sspec = lambda w, blk: pl.BlockSpec((tm, w), srow(blk), pipeline_mode=pl.Buffered(1))
    return pl.pallas_call(
        functools.partial(_proj_out_kernel, n_prompt_tiles=npt),
        out_shape=jax.ShapeDtypeStruct((n_p + n_s, d), F32),
        grid=((n_p + n_s) // tm, d // tn),
        in_specs=[pl.BlockSpec((tm, MLA_W), prow), pl.BlockSpec((tm, FOX_W), prow), pl.BlockSpec((tm, MEM_W), prow),
                  sspec(MLA_W, 0), sspec(FOX_W, MLA_W // FOX_W), sspec(MEM_W, (MLA_W + FOX_W) // MEM_W),
                  pl.BlockSpec((None, MIX_W, tn), lambda i, j: (layer, 0, j))],
        out_specs=pl.BlockSpec((tm, tn), lambda i, j: (i, j)),
        compiler_params=_params("parallel", "arbitrary"),
        name="proj_out",
    )(o_mla_p, o_fox_p, o_mem_p, a_s, a_s, a_s, w_out)


def _rope(x, cos, sin):
    half = QK_ROPE // 2
    rot = jnp.concatenate([-x[:, half:], x[:, :half]], axis=-1)
    return x * cos + rot * sin


def _rms(x, g):
    return x * lax.rsqrt(jnp.mean(x * x, axis=-1, keepdims=True) + RMS_EPS) * g


def _prep_kernel(qa_ref, kva_ref, fk_ref, fv_ref, misc_ref, qg_ref, kg_ref, bf_ref, cos_ref, sin_ref,
                 qan_ref, ckv_ref, krope_ref, kmla_ref, fkv_ref, logf_ref):
    qan_ref[...] = _rms(qa_ref[...], qg_ref[...]).astype(BF16)
    ckv = _rms(kva_ref[...], kg_ref[...])
    ckv_ref[...] = ckv
    misc = misc_ref[...]
    kr = _rope(misc[:, MISC_ROPE:], cos_ref[...], sin_ref[...])
    krope_ref[...] = kr
    kmla_ref[:, :KV_LORA] = ckv.astype(BF16)
    kmla_ref[:, KV_LORA:] = kr.astype(BF16)
    fkv_ref[:, :FOX_HD] = fk_ref[...].astype(BF16)
    fkv_ref[:, FOX_HD:] = fv_ref[...].astype(BF16)
    z = misc + bf_ref[...]
    lf = -(jnp.maximum(-z, 0.0) + jnp.log1p(jnp.exp(-jnp.abs(z))))
    lane = lax.broadcasted_iota(jnp.int32, lf.shape, 1)
    logf_ref[...] = jnp.where(lane < FOX_HEADS, lf, 0.0)


def _prep(h, q_norm, kv_norm, b_f, cos, sin, layer):
    m = h.shape[0]
    tm = _pick(m, 512)
    row = lambda blk: (lambda i: (i, blk))
    par = lambda i: (layer, 0, 0)
    return pl.pallas_call(
        _prep_kernel,
        out_shape=(jax.ShapeDtypeStruct((m, Q_LORA), BF16),
                   jax.ShapeDtypeStruct((m, KV_LORA), F32),
                   jax.ShapeDtypeStruct((m, QK_ROPE), F32),
                   jax.ShapeDtypeStruct((m, MLA_QK), BF16),
                   jax.ShapeDtypeStruct((m, 2 * FOX_HD), BF16),
                   jax.ShapeDtypeStruct((m, LANES), F32)),
        grid=(m // tm,),
        in_specs=[pl.BlockSpec((tm, Q_LORA), row(C_QA // Q_LORA)),
                  pl.BlockSpec((tm, KV_LORA), row(C_KVA // KV_LORA)),
                  pl.BlockSpec((tm, FOX_HD), row(C_FK // FOX_HD)),
                  pl.BlockSpec((tm, FOX_HD), row(C_FV // FOX_HD)),
                  pl.BlockSpec((tm, LANES), row(C_MISC // LANES)),
                  pl.BlockSpec((None, 1, Q_LORA), par),
                  pl.BlockSpec((None, 1, KV_LORA), par),
                  pl.BlockSpec((None, 1, LANES), par),
                  pl.BlockSpec((tm, QK_ROPE), row(0)),
                  pl.BlockSpec((tm, QK_ROPE), row(0))],
        out_specs=(pl.BlockSpec((tm, Q_LORA), row(0)),
                   pl.BlockSpec((tm, KV_LORA), row(0)),
                   pl.BlockSpec((tm, QK_ROPE), row(0)),
                   pl.BlockSpec((tm, MLA_QK), row(0)),
                   pl.BlockSpec((tm, 2 * FOX_HD), row(0)),
                   pl.BlockSpec((tm, LANES), row(0))),
        compiler_params=_params("parallel"),
        name="prep",
    )(h, h, h, h, h, q_norm, kv_norm, b_f, cos, sin)


CUM_BLOCK = 256


def _tri(n, upper):
    r = lax.broadcasted_iota(jnp.int32, (n, n), 0)
    c = lax.broadcasted_iota(jnp.int32, (n, n), 1)
    return jnp.where((r <= c) if upper else (r >= c), 1.0, 0.0).astype(BF16)


def _split3(x):
    x1 = x.astype(BF16)
    r1 = x - x1.astype(F32)
    x2 = r1.astype(BF16)
    x3 = (r1 - x2.astype(F32)).astype(BF16)
    return x1, x2, x3


def _prefix_rows(tri_lower, x):
    return sum(jnp.dot(tri_lower, t, preferred_element_type=F32) for t in _split3(x))


def _prefix_lanes(x, tri_upper):
    return sum(jnp.dot(t, tri_upper, preferred_element_type=F32) for t in _split3(x))


def _cumsum_prompt_kernel(lf_ref, c_ref, ct_ref):
    t = lf_ref.shape[0]
    tri = _tri(CUM_BLOCK, upper=False)
    carry = jnp.zeros((1, LANES), F32)
    for i in range(t // CUM_BLOCK):
        rows = slice(i * CUM_BLOCK, (i + 1) * CUM_BLOCK)
        c = _prefix_rows(tri, lf_ref[rows, :]) + carry
        c_ref[rows, :] = c
        ct_ref[0, :, rows] = c.T[:HEAD_ROWS, :]
        carry = c[CUM_BLOCK - 1:, :]


def _cumsum_prompt(logf, batch, seq):
    return pl.pallas_call(
        _cumsum_prompt_kernel,
        out_shape=(jax.ShapeDtypeStruct((batch * seq, LANES), F32),
                   jax.ShapeDtypeStruct((batch, HEAD_ROWS, seq), F32)),
        grid=(batch,),
        in_specs=[pl.BlockSpec((seq, LANES), lambda b: (b, 0))],
        out_specs=(pl.BlockSpec((seq, LANES), lambda b: (b, 0)),
                   pl.BlockSpec((1, HEAD_ROWS, seq), lambda b: (b, 0, 0))),
        compiler_params=_params("parallel"),
        name="cumsum_prompt",
    )(logf)


def _lane_chunks(s):
    return [s[:, c * LANES:(c + 1) * LANES] for c in range(s.shape[1] // LANES)]


def _flash_update(chunks, v, m_ref, l_ref, acc_ref):
    m_prev = m_ref[...]
    cmax = functools.reduce(jnp.maximum, chunks)
    m_new = jnp.maximum(m_prev, jnp.max(cmax, axis=-1, keepdims=True))
    alpha = jnp.exp(m_prev - m_new)
    ps = [jnp.exp(c - m_new) for c in chunks]
    l_ref[...] = alpha * l_ref[...] + functools.reduce(jnp.add, ps)
    p = jnp.concatenate([x.astype(BF16) for x in ps], axis=1)
    pv = jnp.dot(p, v, preferred_element_type=F32)
    for c in range(acc_ref.shape[1] // LANES):
        cols = slice(c * LANES, (c + 1) * LANES)
        acc_ref[:, cols] = alpha * acc_ref[:, cols] + pv[:, cols]
    m_ref[...] = m_new


def _flash_init(m_ref, l_ref, acc_ref):
    m_ref[...] = jnp.full(m_ref.shape, NEG_BIG, F32)
    l_ref[...] = jnp.zeros_like(l_ref)
    acc_ref[...] = jnp.zeros_like(acc_ref)


def _row_scale(l_ref):
    return 1.0 / jnp.sum(l_ref[...], axis=-1, keepdims=True)


def _nt_dot(a, b):
    return lax.dot_general(a, b, (((1,), (1,)), ((), ())), preferred_element_type=F32)


def _silu(g):
    return g * (1.0 / (1.0 + jnp.exp(-g)))


def _build_mla_queries(q_ref, cos, sin, wuk_ref, qs_ref, rows):
    heads = range(MLA_HEADS)
    ql = [jnp.dot(q_ref[:, h * QK_NOPE:(h + 1) * QK_NOPE].astype(BF16), wuk_ref[h], preferred_element_type=F32)
          for h in heads]
    for h in heads:
        qs_ref[h * rows:(h + 1) * rows, :KV_LORA] = ql[h].astype(BF16)
        off = MLA_HEADS * QK_NOPE + h * QK_ROPE
        qs_ref[h * rows:(h + 1) * rows, KV_LORA:] = _rope(q_ref[:, off:off + QK_ROPE], cos, sin).astype(BF16)


def _finish_mla(acc_ref, l_ref, wuv_ref, g_ref, o_ref, rows):
    inv = _row_scale(l_ref)
    heads = range(MLA_HEADS)
    rsl = [slice(h * rows, (h + 1) * rows) for h in heads]
    o = [jnp.dot((acc_ref[rsl[h], :] * inv[rsl[h], :]).astype(BF16), wuv_ref[h], preferred_element_type=F32)
         for h in heads]
    for h in heads:
        cols = slice(h * MLA_V, (h + 1) * MLA_V)
        o_ref[:, cols] = (o[h] * _silu(g_ref[:, cols])).astype(o_ref.dtype)


def _finish_fox(acc_ref, l_ref, g_ref, o_ref, rows):
    inv = _row_scale(l_ref)
    for h in range(FOX_HEADS):
        sl = slice(h * rows, (h + 1) * rows)
        cols = slice(h * FOX_HD, (h + 1) * FOX_HD)
        o_ref[:, cols] = (acc_ref[sl, :] * inv[sl, :] * _silu(g_ref[:, cols])).astype(o_ref.dtype)


def _causal_mask(s, q0, k0, rows):
    r = lax.broadcasted_iota(jnp.int32, s.shape, 0)
    c = lax.broadcasted_iota(jnp.int32, s.shape, 1)
    return jnp.where(q0 + (r & (rows - 1)) >= k0 + c, s, NEG_BIG)


def _head_rows(x, rows):
    return jnp.concatenate([jnp.broadcast_to(x[h:h + 1, :], (rows, x.shape[1])) for h in range(FOX_HEADS)], axis=0)


PROMPT_TK = 256


def _causal_sweep(qi, tq, tk, logits, update, scores_first):
    n_full = (qi * tq + 1) // tk

    def body(jj, carry):
        if scores_first:
            a = logits(2 * jj, False)
            b = logits(2 * jj + 1, False)
            update(*a)
            update(*b)
        else:
            update(*logits(2 * jj, False))
            update(*logits(2 * jj + 1, False))
        return carry

    lax.fori_loop(0, n_full // 2, body, 0)

    @pl.when(n_full % 2 == 1)
    def _():
        update(*logits(n_full - 1, False))

    update(*logits(n_full, True))


def _mla_prompt_kernel(q_ref, cos_ref, sin_ref, wuk_ref, k_ref, wuv_ref, g_ref, o_ref,
                       qs_ref, m_ref, l_ref, acc_ref):
    qi = pl.program_id(1)
    tq, tk = Q_BLOCK, PROMPT_TK
    _build_mla_queries(q_ref, cos_ref[...], sin_ref[...], wuk_ref, qs_ref, tq)
    _flash_init(m_ref, l_ref, acc_ref)

    def logits(j, masked):
        k = k_ref[pl.ds(pl.multiple_of(j * tk, tk), tk), :]
        s = _nt_dot(qs_ref[...], k) * MLA_SCALE
        if masked:
            s = _causal_mask(s, qi * tq, j * tk, tq)
        return _lane_chunks(s), k[:, :KV_LORA]

    _causal_sweep(qi, tq, tk, logits, functools.partial(_flash_update, m_ref=m_ref, l_ref=l_ref, acc_ref=acc_ref),
                  scores_first=True)
    _finish_mla(acc_ref, l_ref, wuv_ref, g_ref, o_ref, tq)


def _fox_prompt_kernel(q_ref, kv_ref, c_ref, ct_ref, g_ref, o_ref, qs_ref, cq_ref, m_ref, l_ref, acc_ref):
    qi = pl.program_id(1)
    tq, tk = Q_BLOCK, PROMPT_TK
    for h in range(FOX_HEADS):
        rows = slice(h * tq, (h + 1) * tq)
        qs_ref[rows, :] = q_ref[:, h * FOX_HD:(h + 1) * FOX_HD].astype(BF16)
        cq_ref[rows, :] = jnp.broadcast_to(c_ref[:, h:h + 1], (tq, LANES))
    _flash_init(m_ref, l_ref, acc_ref)

    def logits(j, masked):
        k0 = pl.multiple_of(j * tk, tk)
        kv = kv_ref[pl.ds(k0, tk), :]
        s = _nt_dot(qs_ref[...], kv[:, :FOX_HD]) * FOX_SCALE
        ck = _head_rows(ct_ref[0, :, pl.ds(k0, tk)], tq)
        cq = cq_ref[...]
        chunks = [(sc + cq) - kc for sc, kc in zip(_lane_chunks(s), _lane_chunks(ck))]
        if masked:
            chunks = _lane_chunks(_causal_mask(jnp.concatenate(chunks, axis=1), qi * tq, j * tk, tq))
        return chunks, kv[:, FOX_HD:]

    _causal_sweep(qi, tq, tk, logits, functools.partial(_flash_update, m_ref=m_ref, l_ref=l_ref, acc_ref=acc_ref),
                  scores_first=False)
    _finish_fox(acc_ref, l_ref, g_ref, o_ref, tq)


def _mla_prompt(q, cos, sin, wuk, kmla, wuv, h, layer, batch, seq):
    nq = seq // Q_BLOCK
    rows = MLA_HEADS * Q_BLOCK
    tok = lambda blk: (lambda b, i: (b * nq + i, blk))
    par = lambda b, i: (layer, 0, 0, 0)
    return pl.pallas_call(
        _mla_prompt_kernel,
        out_shape=jax.ShapeDtypeStruct((batch * seq, MLA_W), BF16),
        grid=(batch, nq),
        in_specs=[pl.BlockSpec((Q_BLOCK, q.shape[1]), tok(0)),
                  pl.BlockSpec((Q_BLOCK, QK_ROPE), tok(0)),
                  pl.BlockSpec((Q_BLOCK, QK_ROPE), tok(0)),
                  pl.BlockSpec((None,) + wuk.shape[1:], par),
                  pl.BlockSpec((seq, MLA_QK), lambda b, i: (b, 0)),
                  pl.BlockSpec((None,) + wuv.shape[1:], par),
                  pl.BlockSpec((Q_BLOCK, MLA_W), tok(C_G_MLA // MLA_W))],
        out_specs=pl.BlockSpec((Q_BLOCK, MLA_W), tok(0)),
        scratch_shapes=[pltpu.VMEM((rows, MLA_QK), BF16), pltpu.VMEM((rows, LANES), F32),
                        pltpu.VMEM((rows, LANES), F32), pltpu.VMEM((rows, KV_LORA), F32)],
        compiler_params=_params("parallel", "arbitrary"),
        name="mla_prompt",
    )(q, cos, sin, wuk, kmla, wuv, h)


def _fox_prompt(h, fkv, c, ct, batch, seq):
    nq = seq // Q_BLOCK
    rows = FOX_HEADS * Q_BLOCK
    tok = lambda blk: (lambda b, i: (b * nq + i, blk))
    return pl.pallas_call(
        _fox_prompt_kernel,
        out_shape=jax.ShapeDtypeStruct((batch * seq, FOX_W), BF16),
        grid=(batch, nq),
        in_specs=[pl.BlockSpec((Q_BLOCK, FOX_W), tok(C_FQ // FOX_W)),
                  pl.BlockSpec((seq, 2 * FOX_HD), lambda b, i: (b, 0)),
                  pl.BlockSpec((Q_BLOCK, LANES), tok(0)),
                  pl.BlockSpec((1, HEAD_ROWS, seq), lambda b, i: (b, 0, 0)),
                  pl.BlockSpec((Q_BLOCK, FOX_W), tok(C_G_FOX // FOX_W))],
        out_specs=pl.BlockSpec((Q_BLOCK, FOX_W), tok(0)),
        scratch_shapes=[pltpu.VMEM((rows, FOX_HD), BF16), pltpu.VMEM((rows, LANES), F32),
                        pltpu.VMEM((rows, LANES), F32), pltpu.VMEM((rows, LANES), F32),
                        pltpu.VMEM((rows, FOX_HD), F32)],
        compiler_params=_params("parallel", "arbitrary"),
        name="fox_prompt",
    )(h, fkv, c, ct, h)


def _mem_attn_kernel(q_ref, k_ref, v_ref, g_ref, o_ref):
    heads = range(MEM_HEADS)
    cols = [slice(h * MEM_HD, (h + 1) * MEM_HD) for h in heads]
    s = [_nt_dot(q_ref[:, cols[h]].astype(BF16), k_ref[:, cols[h]].astype(BF16)) * MEM_SCALE for h in heads]
    p = [jnp.exp(s[h] - jnp.max(s[h], axis=-1, keepdims=True)) for h in heads]
    o = [jnp.dot(p[h].astype(BF16), v_ref[:, cols[h]].astype(BF16), preferred_element_type=F32) for h in heads]
    for h in heads:
        inv = 1.0 / jnp.sum(p[h], axis=-1, keepdims=True)
        o_ref[:, cols[h]] = (o[h] * inv * _silu(g_ref[:, cols[h]])).astype(o_ref.dtype)


MEM_LANE_CHUNKS = MEM_HD // LANES
MEM_ROWS_PER_SLOT = MEM_LANE_CHUNKS * MEM_HEADS


def _mem_head(ref, h, slots):
    return jnp.concatenate(
        [ref[pl.ds(c * MEM_HEADS + h, slots, stride=MEM_ROWS_PER_SLOT), :] for c in range(MEM_LANE_CHUNKS)], axis=1)


def _mem_attn_rows_kernel(q_ref, k_ref, v_ref, g_ref, o_ref):
    slots = k_ref.shape[0] // MEM_ROWS_PER_SLOT
    heads = range(MEM_HEADS)
    cols = [slice(h * MEM_HD, (h + 1) * MEM_HD) for h in heads]
    s = [_nt_dot(q_ref[:, cols[h]].astype(BF16), _mem_head(k_ref, h, slots).astype(BF16)) * MEM_SCALE for h in heads]
    p = [jnp.exp(s[h] - jnp.max(s[h], axis=-1, keepdims=True)) for h in heads]
    o = [jnp.dot(p[h].astype(BF16), _mem_head(v_ref, h, slots).astype(BF16), preferred_element_type=F32)
         for h in heads]
    for h in heads:
        inv = 1.0 / jnp.sum(p[h], axis=-1, keepdims=True)
        o_ref[:, cols[h]] = (o[h] * inv * _silu(g_ref[:, cols[h]])).astype(o_ref.dtype)


def _mem_cache_rows_view(cache):
    depth, nb, slots = cache.shape[:3]
    v = cache.reshape(depth, nb, slots, MEM_HEADS, MEM_LANE_CHUNKS, LANES)
    return jnp.transpose(v, (0, 1, 2, 4, 3, 5)).reshape(depth, nb * slots * MEM_ROWS_PER_SLOT, LANES)


def _mem_attn_sample(h, k_rows, v_rows, layer, batch, slots, tq, row0):
    r0 = row0 // tq
    tok = lambda blk: (lambda b: (r0 + b, blk))
    cache = pl.BlockSpec((None, slots * MEM_ROWS_PER_SLOT, LANES), lambda b: (layer, b, 0))
    return pl.pallas_call(
        _mem_attn_rows_kernel,
        out_shape=jax.ShapeDtypeStruct((batch * tq, MEM_W), F32),
        grid=(batch,),
        in_specs=[pl.BlockSpec((tq, MEM_W), tok(C_MQ // MEM_W)), cache, cache,
                  pl.BlockSpec((tq, MEM_W), tok(C_G_MEM // MEM_W))],
        out_specs=pl.BlockSpec((tq, MEM_W), lambda b: (b, 0)),
        compiler_params=_params("parallel"),
        name="mem_attn_sample",
    )(h, k_rows, v_rows, h)


def _mem_attn(h, k, v, k_spec, v_spec, batch, tq, steps, row0, out_dtype, name):
    r0 = row0 // tq
    tok = lambda blk: (lambda b, i: (r0 + b * steps + i, blk))
    return pl.pallas_call(
        _mem_attn_kernel,
        out_shape=jax.ShapeDtypeStruct((batch * steps * tq, MEM_W), out_dtype),
        grid=(batch, steps),
        in_specs=[pl.BlockSpec((tq, MEM_W), tok(C_MQ // MEM_W)), k_spec, v_spec,
                  pl.BlockSpec((tq, MEM_W), tok(C_G_MEM // MEM_W))],
        out_specs=pl.BlockSpec((tq, MEM_W), lambda b, i: (b * steps + i, 0)),
        compiler_params=_params("parallel", "arbitrary"),
        name=name,
    )(h, k, v, h)


def _fox_cum_sample_kernel(pt_ref, pool_ref, lfn_ref, ct_ref, cq_ref, buf_ref, sem_ref, tri_ref, earlier_ref,
                           *, layer, n_pages, dec_seq):
    b = pl.program_id(0)
    nb = pl.num_programs(0)
    slot = b % 2
    ppb = CUM_BLOCK // PAGE_SIZE
    n_blk = n_pages // ppb
    past = n_pages * PAGE_SIZE

    def page_copy(bb, i, sl):
        return pltpu.make_async_copy(
            pool_ref.at[layer, pt_ref[bb, i]],
            buf_ref.at[sl, i // ppb, :, pl.ds((i % ppb) * PAGE_SIZE, PAGE_SIZE)],
            sem_ref.at[sl])

    def issue(bb, sl):
        for i in range(n_pages):
            page_copy(bb, i, sl).start()

    rows = n_blk * HEAD_ROWS

    @pl.when(b == 0)
    def _():
        issue(0, 0)
        tri_ref[...] = _tri(CUM_BLOCK, upper=True)
        r = lax.broadcasted_iota(jnp.int32, (rows, rows), 0)
        c = lax.broadcasted_iota(jnp.int32, (rows, rows), 1)
        earlier_ref[...] = jnp.where(
            ((r & (HEAD_ROWS - 1)) == (c & (HEAD_ROWS - 1))) & (c < r - (r & (HEAD_ROWS - 1))), 1.0, 0.0).astype(BF16)

    @pl.when(b + 1 < nb)
    def _():
        issue(b + 1, 1 - slot)

    for i in range(n_pages):
        page_copy(b, i, slot).wait()

    x = buf_ref[slot].reshape(rows, CUM_BLOCK)
    p = _prefix_lanes(x, tri_ref[...])
    tot = jnp.broadcast_to(p[:, CUM_BLOCK - 1:], (rows, LANES))
    earlier = earlier_ref[...]
    off = sum(jnp.dot(earlier, t, preferred_element_type=F32) for t in _split3(tot))
    cum = p + off[:, :1]
    for blk in range(n_blk):
        ct_ref[0, :, blk * CUM_BLOCK:(blk + 1) * CUM_BLOCK] = cum[blk * HEAD_ROWS:(blk + 1) * HEAD_ROWS, :]
    p_tot = cum[rows - HEAD_ROWS:, CUM_BLOCK - 1:]

    pad_rows = lambda a: jnp.concatenate([a, jnp.zeros((LANES - a.shape[0], LANES), F32)], axis=0)
    lfn_t = pad_rows(lfn_ref[...]).T[:HEAD_ROWS, :]
    c_new_t = _prefix_lanes(lfn_t, tri_ref[:LANES, :LANES]) + p_tot
    ct_ref[0, :, past:past + LANES] = c_new_t
    c_new = pad_rows(c_new_t).T
    cq_ref[0] = jnp.concatenate(
        [jnp.broadcast_to(c_new[:dec_seq, h:h + 1], (dec_seq, LANES)) for h in range(FOX_HEADS)], axis=0)


def _fox_cum_sample(page_table, pool_t, logf, layer, row0, dec_seq):
    dec_batch, n_pages = page_table.shape
    past = n_pages * PAGE_SIZE
    n_blk = past // CUM_BLOCK
    kern = functools.partial(_fox_cum_sample_kernel, layer=layer, n_pages=n_pages, dec_seq=dec_seq)
    return pl.pallas_call(
        kern,
        out_shape=(jax.ShapeDtypeStruct((dec_batch, HEAD_ROWS, past + LANES), F32),
                   jax.ShapeDtypeStruct((dec_batch, FOX_HEADS * dec_seq, LANES), F32)),
        grid_spec=pltpu.PrefetchScalarGridSpec(
            num_scalar_prefetch=1,
            grid=(dec_batch,),
            in_specs=[pl.BlockSpec(memory_space=pl.ANY),
                      pl.BlockSpec((dec_seq, LANES), lambda b, pt: (row0 // dec_seq + b, 0))],
            out_specs=(pl.BlockSpec((1, HEAD_ROWS, past + LANES), lambda b, pt: (b, 0, 0)),
                       pl.BlockSpec((1, FOX_HEADS * dec_seq, LANES), lambda b, pt: (b, 0, 0))),
            scratch_shapes=[pltpu.VMEM((2, n_blk, HEAD_ROWS, CUM_BLOCK), F32), pltpu.SemaphoreType.DMA((2,)),
                            pltpu.VMEM((CUM_BLOCK, CUM_BLOCK), BF16),
                            pltpu.VMEM((n_blk * HEAD_ROWS, n_blk * HEAD_ROWS), BF16)]),
        compiler_params=_params("arbitrary"),
        name="fox_cum_sample",
    )(page_table, pool_t, logf)


PAGES_PER_CHUNK = 64
SAMPLE_SUB_TILE = 4096


def _sample_attn_kernel(pt_ref, q_ref, cos_ref, sin_ref, wuk_ref, wuv_ref, fq_ref,
                        ckvn_ref, krn_ref, fkn_ref, fvn_ref, ct_ref, ctn_ref, cq_ref, gm_ref, gf_ref,
                        ckv_pool, krt_pool, fk_pool, fv_pool,
                        om_ref, of_ref,
                        ckv_buf, krt_buf, fk_buf, fv_buf, sem_ref,
                        qm_ref, qf_ref, m1_ref, l1_ref, a1_ref, m2_ref, l2_ref, a2_ref,
                        *, layer, n_chunks, ppc, dec_seq):
    b = pl.program_id(0)
    j = pl.program_id(1)
    g = b * n_chunks + j
    total = pl.num_programs(0) * n_chunks
    slot = g % 2

    def page_copies(bb, jj, i, sl):
        pid = pt_ref[bb, jj * ppc + i]
        rows = pl.ds(i * PAGE_SIZE, PAGE_SIZE)
        return (pltpu.make_async_copy(ckv_pool.at[layer, pid], ckv_buf.at[sl, rows, :], sem_ref.at[0, sl]),
                pltpu.make_async_copy(krt_pool.at[layer, pid], krt_buf.at[sl, :, rows], sem_ref.at[1, sl]),
                pltpu.make_async_copy(fk_pool.at[layer, pid], fk_buf.at[sl, rows, :], sem_ref.at[2, sl]),
                pltpu.make_async_copy(fv_pool.at[layer, pid], fv_buf.at[sl, rows, :], sem_ref.at[3, sl]))

    def issue(bb, jj, sl):
        for i in range(ppc):
            for cp in page_copies(bb, jj, i, sl):
                cp.start()

    @pl.when(g == 0)
    def _():
        issue(0, 0, 0)

    @pl.when(g + 1 < total)
    def _():
        nxt = g + 1
        issue(nxt // n_chunks, nxt % n_chunks, 1 - slot)

    @pl.when(j == 0)
    def _():
        _build_mla_queries(q_ref, cos_ref[...], sin_ref[...], wuk_ref, qm_ref, dec_seq)
        for h in range(FOX_HEADS):
            qf_ref[h * dec_seq:(h + 1) * dec_seq, :] = fq_ref[:, h * FOX_HD:(h + 1) * FOX_HD].astype(BF16)
        _flash_init(m1_ref, l1_ref, a1_ref)
        _flash_init(m2_ref, l2_ref, a2_ref)

    for i in range(ppc):
        for cp in page_copies(b, j, i, slot):
            cp.wait()

    def fox_logits(fk, ck):
        s = _nt_dot(qf_ref[...], fk) * FOX_SCALE
        cq = cq_ref[0]
        return [(sc + cq) - kc for sc, kc in zip(_lane_chunks(s), _lane_chunks(_head_rows(ck, dec_seq)))]

    qm = qm_ref[...]
    sub = min(SAMPLE_SUB_TILE, ppc * PAGE_SIZE)
    for t in range(ppc * PAGE_SIZE // sub):
        keys = slice(t * sub, (t + 1) * sub)
        ckv = ckv_buf[slot, keys, :].astype(BF16)
        s1 = (_nt_dot(qm[:, :KV_LORA], ckv)
              + jnp.dot(qm[:, KV_LORA:], krt_buf[slot, :, keys].astype(BF16), preferred_element_type=F32)) * MLA_SCALE
        _flash_update(_lane_chunks(s1), ckv, m1_ref, l1_ref, a1_ref)
        _flash_update(fox_logits(fk_buf[slot, keys, :].astype(BF16), ct_ref[0, :, keys]),
                      fv_buf[slot, keys, :].astype(BF16), m2_ref, l2_ref, a2_ref)

    @pl.when(j == n_chunks - 1)
    def _():
        pad = lambda x: jnp.concatenate(
            [x, jnp.zeros((LANES - dec_seq, x.shape[1]), x.dtype)], axis=0).astype(BF16)
        rows = FOX_HEADS * dec_seq
        r = lax.broadcasted_iota(jnp.int32, (rows, LANES), 0)
        c = lax.broadcasted_iota(jnp.int32, (rows, LANES), 1)
        visible = c <= (r & (dec_seq - 1))
        ckv_n = pad(ckvn_ref[...])
        qmm = qm_ref[...]
        s1n = (_nt_dot(qmm[:, :KV_LORA], ckv_n) + _nt_dot(qmm[:, KV_LORA:], pad(krn_ref[...]))) * MLA_SCALE
        _flash_update([jnp.where(visible, s1n, NEG_BIG)], ckv_n, m1_ref, l1_ref, a1_ref)
        s2n, = fox_logits(pad(fkn_ref[...]), ctn_ref[0])
        _flash_update([jnp.where(visible, s2n, NEG_BIG)], pad(fvn_ref[...]), m2_ref, l2_ref, a2_ref)
        _finish_mla(a1_ref, l1_ref, wuv_ref, gm_ref, om_ref, dec_seq)
        _finish_fox(a2_ref, l2_ref, gf_ref, of_ref, dec_seq)


def _sample_attn(page_table, q, cos, sin, wuk, wuv, h, ckv, krope, ct, cq,
                 ckv_pool, krt_pool, fk_pool, fv_pool, layer, row0, dec_seq):
    dec_batch, n_pages = page_table.shape
    ppc = PAGES_PER_CHUNK if n_pages % PAGES_PER_CHUNK == 0 else n_pages
    n_chunks = n_pages // ppc
    tkc = ppc * PAGE_SIZE
    rows = MLA_HEADS * dec_seq
    r0 = row0 // dec_seq
    tok = lambda blk: (lambda b, j, pt: (r0 + b, blk))
    par = lambda b, j, pt: (layer, 0, 0, 0)
    anyspec = pl.BlockSpec(memory_space=pl.ANY)
    kern = functools.partial(_sample_attn_kernel, layer=layer, n_chunks=n_chunks, ppc=ppc, dec_seq=dec_seq)
    return pl.pallas_call(
        kern,
        out_shape=(jax.ShapeDtypeStruct((dec_batch * dec_seq, MLA_W), F32),
                   jax.ShapeDtypeStruct((dec_batch * dec_seq, FOX_W), F32)),
        grid_spec=pltpu.PrefetchScalarGridSpec(
            num_scalar_prefetch=1,
            grid=(dec_batch, n_chunks),
            in_specs=[pl.BlockSpec((dec_seq, q.shape[1]), tok(0)),
                      pl.BlockSpec((dec_seq, QK_ROPE), tok(0)),
                      pl.BlockSpec((dec_seq, QK_ROPE), tok(0)),
                      pl.BlockSpec((None,) + wuk.shape[1:], par),
                      pl.BlockSpec((None,) + wuv.shape[1:], par),
                      pl.BlockSpec((dec_seq, FOX_W), tok(C_FQ // FOX_W)),
                      pl.BlockSpec((dec_seq, KV_LORA), tok(0)),
                      pl.BlockSpec((dec_seq, QK_ROPE), tok(0)),
                      pl.BlockSpec((dec_seq, FOX_HD), tok(C_FK // FOX_HD)),
                      pl.BlockSpec((dec_seq, FOX_HD), tok(C_FV // FOX_HD)),
                      pl.BlockSpec((1, HEAD_ROWS, tkc), lambda b, j, pt: (b, 0, j)),
                      pl.BlockSpec((1, HEAD_ROWS, LANES), lambda b, j, pt: (b, 0, n_pages)),
                      pl.BlockSpec((1, rows, LANES), lambda b, j, pt: (b, 0, 0)),
                      pl.BlockSpec((dec_seq, MLA_W), tok(C_G_MLA // MLA_W)),
                      pl.BlockSpec((dec_seq, FOX_W), tok(C_G_FOX // FOX_W)),
                      anyspec, anyspec, anyspec, anyspec],
            out_specs=(pl.BlockSpec((dec_seq, MLA_W), lambda b, j, pt: (b, 0)),
                       pl.BlockSpec((dec_seq, FOX_W), lambda b, j, pt: (b, 0))),
            scratch_shapes=[pltpu.VMEM((2, tkc, KV_LORA), F32), pltpu.VMEM((2, QK_ROPE, tkc), F32),
                            pltpu.VMEM((2, tkc, FOX_HD), F32), pltpu.VMEM((2, tkc, FOX_HD), F32),
                            pltpu.SemaphoreType.DMA((4, 2)),
                            pltpu.VMEM((rows, MLA_QK), BF16), pltpu.VMEM((rows, FOX_HD), BF16),
                            pltpu.VMEM((rows, LANES), F32), pltpu.VMEM((rows, LANES), F32),
                            pltpu.VMEM((rows, KV_LORA), F32),
                            pltpu.VMEM((rows, LANES), F32), pltpu.VMEM((rows, LANES), F32),
                            pltpu.VMEM((rows, FOX_HD), F32)]),
        compiler_params=_params("arbitrary", "arbitrary"),
        name="sample_attn",
    )(page_table, q, cos, sin, wuk, wuv, h, ckv, krope, h, h, ct, ct, cq, h, h,
      ckv_pool, krt_pool, fk_pool, fv_pool)


def _ln_kernel(x_ref, y_ref, g_ref, b_ref, o_ref, ob_ref, *, alpha):
    v = alpha * x_ref[...] + y_ref[...]
    mu = jnp.mean(v, axis=-1, keepdims=True)
    vc = v - mu
    var = jnp.mean(vc * vc, axis=-1, keepdims=True)
    out = vc * lax.rsqrt(var + LN_EPS) * g_ref[...] + b_ref[...]
    o_ref[...] = out
    ob_ref[...] = out.astype(BF16)


def _layernorm(x, y, g, b, layer, alpha):
    m, d = x.shape
    tm = _pick(m, 256)
    row = lambda i: (i, 0)
    par = lambda i: (layer, 0, 0)
    return pl.pallas_call(
        functools.partial(_ln_kernel, alpha=alpha),
        out_shape=(jax.ShapeDtypeStruct((m, d), F32), jax.ShapeDtypeStruct((m, d), BF16)),
        grid=(m // tm,),
        in_specs=[pl.BlockSpec((tm, d), row), pl.BlockSpec((tm, d), row),
                  pl.BlockSpec((None, 1, d), par), pl.BlockSpec((None, 1, d), par)],
        out_specs=(pl.BlockSpec((tm, d), row), pl.BlockSpec((tm, d), row)),
        compiler_params=_params("parallel"),
        name="residual_layernorm",
    )(x, y, g, b)


def _split_cols(w):
    offs = [0]
    for s in IN_SIZES:
        offs.append(offs[-1] + s)
    return [w[..., offs[i]:offs[i + 1]] for i in range(len(IN_SIZES))]


def _arrange_w_in(w_in):
    q_a, kv_a, k_pe, fq, fk, fv, f_logit, mq, gate = _split_cols(w_in)
    lead = w_in.shape[:-1]
    zeros = lambda n: jnp.zeros(lead + (n,), w_in.dtype)
    misc = jnp.concatenate([f_logit, zeros(LANES - FOX_HEADS - QK_ROPE), k_pe], axis=-1)
    parts = [fq, gate[..., :MLA_W], gate[..., MLA_W:MLA_W + FOX_W], q_a, kv_a, fk, fv, misc, zeros(LANES),
             gate[..., MLA_W + FOX_W:], mq]
    out = jnp.concatenate(parts, axis=-1).astype(BF16)
    assert out.shape[-1] == N_PROJ
    return out


def _arrange_w_q_up(w_q_up):
    depth, lora, _ = w_q_up.shape
    w = w_q_up.reshape(depth, lora, MLA_HEADS, QK_NOPE + QK_ROPE)
    nope = w[..., :QK_NOPE].reshape(depth, lora, MLA_HEADS * QK_NOPE)
    rope = w[..., QK_NOPE:].reshape(depth, lora, MLA_HEADS * QK_ROPE)
    return jnp.concatenate([nope, rope], axis=-1).astype(BF16)


def _rope_tables(positions):
    half = QK_ROPE // 2
    inv = ROPE_THETA ** (-jnp.arange(half, dtype=F32) / half)
    ang = positions.astype(F32)[:, None] * inv[None, :]
    cos, sin = jnp.cos(ang), jnp.sin(ang)
    return jnp.concatenate([cos, cos], axis=-1), jnp.concatenate([sin, sin], axis=-1)


def kernel(x_prompt, x_sample, cache_mla_ckv, cache_mla_krope, cache_fox_k, cache_fox_v, cache_fox_logf,
           cache_mem_k, cache_mem_v, page_table, mem_prompt, w_in, b_f, q_a_norm, w_q_up, kv_a_norm, w_kv_up,
           w_mem_k, w_mem_v, w_out, ln_g, ln_b):
    batch, seq, d_model = x_prompt.shape
    dec_batch, dec_seq, _ = x_sample.shape
    depth = w_in.shape[0]
    n_pages = page_table.shape[1]
    past = n_pages * PAGE_SIZE
    mem_slots = mem_prompt.shape[1]
    assert w_in.shape[-1] == sum(IN_SIZES) and cache_mla_ckv.shape[2] == PAGE_SIZE
    assert seq % PROMPT_TK == 0 and PROMPT_TK % Q_BLOCK == 0 and dec_seq % SUBLANES == 0
    assert past % CUM_BLOCK == 0 and CUM_BLOCK % PAGE_SIZE == 0
    alpha = (2 * depth) ** 0.25
    n_p = batch * seq
    n_s = dec_batch * dec_seq

    w_in_r = _arrange_w_in(w_in)
    w_q_r = _arrange_w_q_up(w_q_up)
    wuk = jnp.transpose(w_kv_up[..., :QK_NOPE], (0, 2, 3, 1)).astype(BF16)
    wuv = jnp.transpose(w_kv_up[..., QK_NOPE:], (0, 2, 1, 3)).astype(BF16)
    w_mem = jnp.concatenate([w_mem_k, w_mem_v], axis=-1).astype(BF16)
    w_out_b = w_out.astype(BF16)
    b_f_pad = jnp.pad(b_f, ((0, 0), (0, LANES - FOX_HEADS)))[:, None, :]
    q_norm, kv_norm = q_a_norm[:, None, :], kv_a_norm[:, None, :]
    ln_g3, ln_b3 = ln_g[:, None, :], ln_b[:, None, :]
    mem_b = mem_prompt.reshape(batch * mem_slots, d_model).astype(BF16)
    krt_pool = jnp.swapaxes(cache_mla_krope, 2, 3)
    lft_pool = jnp.swapaxes(
        jnp.pad(cache_fox_logf, ((0, 0), (0, 0), (0, 0), (0, HEAD_ROWS - FOX_HEADS))), 2, 3)
    mem_k_s = _mem_cache_rows_view(cache_mem_k)
    mem_v_s = _mem_cache_rows_view(cache_mem_v)

    pos = jnp.concatenate([jnp.tile(jnp.arange(seq), batch), jnp.tile(past + jnp.arange(dec_seq), dec_batch)])
    cos, sin = _rope_tables(pos)

    x = jnp.concatenate([x_prompt.reshape(n_p, d_model), x_sample.reshape(n_s, d_model)], axis=0)
    xb = x.astype(BF16)

    tq_mem = _pick(seq, 512)
    outs = {k: [] for k in ("p_ckv", "p_kr", "p_fk", "p_fv", "p_lf", "p_mk", "p_mv",
                            "s_ckv", "s_kr", "s_fk", "s_fv", "s_lf")}
    for l in range(depth):
        h = _linear(xb, w_in_r, l, F32, "proj_in")
        qan, ckv, krope, kmla, fkv, logf = _prep(h, q_norm, kv_norm, b_f_pad, cos, sin, l)
        q = _linear(qan, w_q_r, l, F32, "q_up")

        c, ct = _cumsum_prompt(logf, batch, seq)
        o_mla_p = _mla_prompt(q, cos, sin, wuk, kmla, wuv, h, l, batch, seq)
        o_fox_p = _fox_prompt(h, fkv, c, ct, batch, seq)
        mkv = _linear(mem_b, w_mem, l, F32, "mem_kv")
        o_mem_p = _mem_attn(h, mkv, mkv,
                            pl.BlockSpec((mem_slots, MEM_W), lambda b, i: (b, 0)),
                            pl.BlockSpec((mem_slots, MEM_W), lambda b, i: (b, 1)),
                            batch, tq_mem, seq // tq_mem, 0, BF16, "mem_attn_prompt")

        ct_s, cq_s = _fox_cum_sample(page_table, lft_pool, logf, l, n_p, dec_seq)
        o_mla_s, o_fox_s = _sample_attn(page_table, q, cos, sin, wuk, wuv, h, ckv, krope, ct_s, cq_s,
                                        cache_mla_ckv, krt_pool, cache_fox_k, cache_fox_v, l, n_p, dec_seq)
        o_mem_s = _mem_attn_sample(h, mem_k_s, mem_v_s, l, dec_batch, mem_slots, dec_seq, n_p)
        a_s = jnp.concatenate([o_mla_s, o_fox_s, o_mem_s], axis=1).astype(BF16)

        y = _proj_out(o_mla_p, o_fox_p, o_mem_p, a_s, w_out_b, l)
        x, xb = _layernorm(x, y, ln_g3, ln_b3, l, alpha)

        fk, fv = h[:, C_FK:C_FK + FOX_HD], h[:, C_FV:C_FV + FOX_HD]
        lf = logf[:, :FOX_HEADS]
        outs["p_ckv"].append(ckv[:n_p].reshape(batch, seq, KV_LORA))
        outs["p_kr"].append(krope[:n_p].reshape(batch, seq, QK_ROPE))
        outs["p_fk"].append(fk[:n_p].reshape(batch, seq, FOX_HD))
        outs["p_fv"].append(fv[:n_p].reshape(batch, seq, FOX_HD))
        outs["p_lf"].append(lf[:n_p].reshape(batch, seq, FOX_HEADS))
        outs["p_mk"].append(mkv[:, :MEM_W].reshape(batch, mem_slots, MEM_HEADS, MEM_HD))
        outs["p_mv"].append(mkv[:, MEM_W:].reshape(batch, mem_slots, MEM_HEADS, MEM_HD))
        outs["s_ckv"].append(ckv[n_p:].reshape(dec_batch, dec_seq, KV_LORA))
        outs["s_kr"].append(krope[n_p:].reshape(dec_batch, dec_seq, QK_ROPE))
        outs["s_fk"].append(fk[n_p:].reshape(dec_batch, dec_seq, FOX_HD))
        outs["s_fv"].append(fv[n_p:].reshape(dec_batch, dec_seq, FOX_HD))
        outs["s_lf"].append(lf[n_p:].reshape(dec_batch, dec_seq, FOX_HEADS))

    st = {k: jnp.stack(v) for k, v in outs.items()}
    return (x[:n_p].reshape(batch, seq, d_model), x[n_p:].reshape(dec_batch, dec_seq, d_model),
            st["p_ckv"], st["p_kr"], st["p_fk"], st["p_fv"], st["p_lf"], st["p_mk"], st["p_mv"],
            st["s_ckv"], st["s_kr"], st["s_fk"], st["s_fv"], st["s_lf"])
```
